```python
import jax, jax.numpy as jnp
from jax import lax
import numpy as np

D_MODEL = 1024
BATCH = 8
SEQ = 2048
DEPTH = 4
DEC_BATCH = 32
DEC_SEQ = 1
PAST_LEN = 8192
PAGE_SIZE = 128

N_MIXERS = 2
N_RET_LAYERS = (DEPTH + 1) // 2
N_SB_LAYERS = DEPTH // 2
N_META = 16
RET_HEADS = 4
RET_QK_DIM = D_MODEL // RET_HEADS
RET_V_DIM = 2 * RET_QK_DIM
RET_CHUNK = 128
RET_IN_WIDTH = RET_HEADS * (2 * RET_QK_DIM + 2 * RET_V_DIM)
ROPE_BASE = 10000.0
SB_HEADS = 16
SB_HEAD_DIM = D_MODEL // SB_HEADS
SB_BLOCK = 128
SB_BIAS_INIT = -6.0
FFN_HIDDEN = -(-(8 * D_MODEL) // (3 * 256)) * 256
DEEPNORM_ALPHA = (2 * DEPTH) ** 0.25
DEEPNORM_BETA = (8 * DEPTH) ** -0.25
LN_EPS = 1e-5

kernel_name = "retnet_stickbreaking_hybrid_step"


def layer_norm(x, g, b):
    xf = x.astype(jnp.float32)
    mu = jnp.mean(xf, axis=-1, keepdims=True)
    var = jnp.mean(jnp.square(xf - mu), axis=-1, keepdims=True)
    return ((xf - mu) * lax.rsqrt(var + LN_EPS)).astype(x.dtype) * g + b


def swiglu_ffn(h, w_in, w_out):
    a, b = jnp.split(h @ w_in, 2, axis=-1)
    return (jax.nn.silu(a) * b) @ w_out


def post_norm_block(h, m, w_ffn_in, w_ffn_out, g1, b1, g2, b2):
    h = layer_norm(DEEPNORM_ALPHA * h + m, g1, b1)
    return layer_norm(DEEPNORM_ALPHA * h + swiglu_ffn(h, w_ffn_in, w_ffn_out), g2, b2)


def rotary(x, pos):
    half = x.shape[-1] // 2
    inv = ROPE_BASE ** (-jnp.arange(half, dtype=jnp.float32) / half)
    ang = pos.astype(jnp.float32)[:, None] * inv[None, :]
    cos = jnp.cos(ang)[None, :, None, :].astype(x.dtype)
    sin = jnp.sin(ang)[None, :, None, :].astype(x.dtype)
    x1, x2 = x[..., :half], x[..., half:]
    return jnp.concatenate([x1 * cos - x2 * sin, x1 * sin + x2 * cos], axis=-1)


def retention_log_decay():
    return jnp.log(1.0 - 2.0 ** (-5.0 - jnp.arange(RET_HEADS, dtype=jnp.float32)))


def retention_project(h, pos, w_in):
    B, T, _ = h.shape
    qk = RET_HEADS * RET_QK_DIM
    q, k, v, g = jnp.split(h @ w_in, [qk, 2 * qk, 2 * qk + RET_HEADS * RET_V_DIM], axis=-1)
    q = rotary(q.reshape(B, T, RET_HEADS, RET_QK_DIM), pos)
    k = rotary(k.reshape(B, T, RET_HEADS, RET_QK_DIM), pos) * (RET_QK_DIM ** -0.5)
    v = v.reshape(B, T, RET_HEADS, RET_V_DIM)
    to_bhtd = lambda t: jnp.transpose(t, (0, 2, 1, 3))
    return to_bhtd(q), to_bhtd(k), to_bhtd(v), g


def retention_chunk(state, q, k, v, log_g):
    C = q.shape[2]
    idx = jnp.arange(C, dtype=jnp.float32)
    diff = idx[:, None] - idx[None, :]
    lg = log_g[:, None, None]
    decay = jnp.where(diff >= 0, jnp.exp(lg * jnp.maximum(diff, 0.0)), 0.0).astype(q.dtype)
    q_decay = jnp.exp(log_g[:, None] * (idx[None, :] + 1.0)).astype(q.dtype)
    k_decay = jnp.exp(log_g[:, None] * (C - 1.0 - idx[None, :])).astype(q.dtype)
    chunk_decay = jnp.exp(log_g * C).astype(q.dtype)
    scores = jnp.einsum('bhtd,bhsd->bhts', q, k) * decay[None]
    o = jnp.einsum('bhts,bhsv->bhtv', scores, v) + jnp.einsum('bhtd,bhdv->bhtv', q * q_decay[None, :, :, None], state)
    new_state = chunk_decay[None, :, None, None] * state + jnp.einsum('bhsd,bhsv->bhdv', k * k_decay[None, :, :, None], v)
    return new_state, o


def retention_output(o, g, w_out):
    of = o.astype(jnp.float32)
    o = (of * lax.rsqrt(jnp.mean(jnp.square(of), axis=-1, keepdims=True) + LN_EPS)).astype(o.dtype)
    B, H, T, dv = o.shape
    o = jnp.transpose(o, (0, 2, 1, 3)).reshape(B, T, H * dv)
    return (jax.nn.silu(g) * o) @ w_out


def retention_prompt(h, pos, w_in, w_out):
    q, k, v, g = retention_project(h, pos, w_in)
    log_g = retention_log_decay()
    B = h.shape[0]
    s0 = jnp.zeros((B, RET_HEADS, RET_QK_DIM, RET_V_DIM), q.dtype)
    s_meta, o_meta = retention_chunk(s0, q[:, :, :N_META], k[:, :, :N_META], v[:, :, :N_META], log_g)

    def to_chunks(t):
        d = t.shape[-1]
        n = (t.shape[2] - N_META) // RET_CHUNK
        return jnp.moveaxis(t[:, :, N_META:].reshape(B, RET_HEADS, n, RET_CHUNK, d), 2, 0)

    def step(s, qkv):
        return retention_chunk(s, qkv[0], qkv[1], qkv[2], log_g)

    s_final, o_chunks = lax.scan(step, s_meta, (to_chunks(q), to_chunks(k), to_chunks(v)))
    o_real = jnp.moveaxis(o_chunks, 0, 2).reshape(B, RET_HEADS, -1, RET_V_DIM)
    o = jnp.concatenate([o_meta, o_real], axis=2)
    return retention_output(o, g, w_out), s_final


def retention_sample(h, pos, state, w_in, w_out):
    q, k, v, g = retention_project(h, pos, w_in)
    s_new, o = retention_chunk(state, q, k, v, retention_log_decay())
    return retention_output(o, g, w_out), s_new


def sb_project(h, w_qkv):
    B, T, _ = h.shape
    q, k, v = jnp.split(h @ w_qkv, 3, axis=-1)
    shp = (B, T, SB_HEADS, SB_HEAD_DIM)
    return q.reshape(shp), k.reshape(shp), v.reshape(shp)


def stick_breaking(q, k, v, q_pos, k_pos, bias):
    z = jnp.einsum('bthd,bshd->bhts', q, k).astype(jnp.float32) * (SB_HEAD_DIM ** -0.5)
    z = z + bias.astype(jnp.float32)[None, :, None, None]
    visible = k_pos[None, :] < q_pos[:, None]
    log_keep = jnp.where(visible, jax.nn.log_sigmoid(-z), 0.0)
    log_later = lax.cumsum(log_keep, axis=3, reverse=True) - log_keep
    a = jnp.where(visible, jnp.exp(jax.nn.log_sigmoid(z) + log_later), 0.0)
    return jnp.einsum('bhts,bshd->bthd', a.astype(v.dtype), v)


def sb_prompt(q, k, v, bias):
    B, L, H, d = q.shape
    pos = jnp.arange(L)
    o_meta = stick_breaking(q[:, :N_META], k, v, pos[:N_META], pos, bias)
    n = (L - N_META) // SB_BLOCK
    q_blocks = jnp.moveaxis(q[:, N_META:].reshape(B, n, SB_BLOCK, H, d), 1, 0)
    p_blocks = pos[N_META:].reshape(n, SB_BLOCK)
    o_blocks = lax.map(lambda qp: stick_breaking(qp[0], k, v, qp[1], pos, bias), (q_blocks, p_blocks))
    o_real = jnp.moveaxis(o_blocks, 0, 1).reshape(B, L - N_META, H, d)
    return jnp.concatenate([o_meta, o_real], axis=1)


def setup_inputs(seed: int = 0) -> dict:
    key = jax.random.key(seed)
    ks = jax.random.split(key, 18)
    n_pages = PAST_LEN // PAGE_SIZE
    n_used = DEC_BATCH * n_pages
    n_pool = n_used + (n_used + 3) // 4
    page_table = jax.random.permutation(ks[5], n_pool)[:n_used].reshape(DEC_BATCH, n_pages).astype(jnp.int32)
    d_v_total = RET_HEADS * RET_V_DIM
    return {
        "x_prompt": jax.random.normal(ks[0], (BATCH, SEQ, D_MODEL), jnp.float32),
        "x_sample": jax.random.normal(ks[1], (DEC_BATCH, DEC_SEQ, D_MODEL), jnp.float32),
        "state_ret": 0.5 * jax.random.normal(ks[2], (N_RET_LAYERS, DEC_BATCH, RET_HEADS, RET_QK_DIM, RET_V_DIM), jnp.float32),
        "cache_k": jax.random.normal(ks[3], (N_SB_LAYERS, n_pool, PAGE_SIZE, SB_HEADS, SB_HEAD_DIM), jnp.float32),
        "cache_v": jax.random.normal(ks[4], (N_SB_LAYERS, n_pool, PAGE_SIZE, SB_HEADS, SB_HEAD_DIM), jnp.float32),
        "page_table": page_table,
        "meta_tokens": jax.random.normal(ks[6], (N_META, D_MODEL), jnp.float32),
        "w_ret_in": jax.random.normal(ks[7], (N_RET_LAYERS, D_MODEL, RET_IN_WIDTH), jnp.float32) * D_MODEL ** -0.5,
        "w_ret_out": jax.random.normal(ks[8], (N_RET_LAYERS, d_v_total, D_MODEL), jnp.float32) * (d_v_total ** -0.5 * DEEPNORM_BETA),
        "w_sb_qkv": jax.random.normal(ks[9], (N_SB_LAYERS, D_MODEL, 3 * D_MODEL), jnp.float32) * D_MODEL ** -0.5,
        "w_sb_out": jax.random.normal(ks[10], (N_SB_LAYERS, D_MODEL, D_MODEL), jnp.float32) * (D_MODEL ** -0.5 * DEEPNORM_BETA),
        "sb_bias": SB_BIAS_INIT + 0.5 * jax.random.normal(ks[16], (N_SB_LAYERS, SB_HEADS), jnp.float32),
        "w_ffn_in": jax.random.normal(ks[11], (DEPTH, D_MODEL, 2 * FFN_HIDDEN), jnp.float32) * D_MODEL ** -0.5,
        "w_ffn_out": jax.random.normal(ks[12], (DEPTH, FFN_HIDDEN, D_MODEL), jnp.float32) * (FFN_HIDDEN ** -0.5 * DEEPNORM_BETA),
        "ln_mix_g": 1.0 + 0.05 * jax.random.normal(ks[13], (DEPTH, D_MODEL), jnp.float32),
        "ln_mix_b": 0.02 * jax.random.normal(ks[14], (DEPTH, D_MODEL), jnp.float32),
        "ln_ffn_g": 1.0 + 0.05 * jax.random.normal(ks[15], (DEPTH, D_MODEL), jnp.float32),
        "ln_ffn_b": 0.02 * jax.random.normal(ks[17], (DEPTH, D_MODEL), jnp.float32),
    }


def reference(x_prompt, x_sample, state_ret, cache_k, cache_v, page_table, meta_tokens,
              w_ret_in, w_ret_out, w_sb_qkv, w_sb_out, sb_bias, w_ffn_in, w_ffn_out,
              ln_mix_g, ln_mix_b, ln_ffn_g, ln_ffn_b):
    B = x_prompt.shape[0]
    n_dec = page_table.shape[0]
    past_len = page_table.shape[1] * PAGE_SIZE
    meta = jnp.broadcast_to(meta_tokens[None].astype(x_prompt.dtype), (B, N_META, D_MODEL))
    hp = jnp.concatenate([meta, x_prompt], axis=1)
    hs = x_sample
    pos_p = jnp.arange(hp.shape[1])
    pos_s = past_len + jnp.arange(hs.shape[1])

    ret_p, ret_s, kp_rows, vp_rows, ks_rows, vs_rows = [], [], [], [], [], []
    for i in range(DEPTH):
        j = i // N_MIXERS
        if i % N_MIXERS == 0:
            mp, sp = retention_prompt(hp, pos_p, w_ret_in[j], w_ret_out[j])
            ms, ss = retention_sample(hs, pos_s, state_ret[j], w_ret_in[j], w_ret_out[j])
            ret_p.append(sp)
            ret_s.append(ss)
        else:
            qp, kp, vp = sb_project(hp, w_sb_qkv[j])
            op = sb_prompt(qp, kp, vp, sb_bias[j])
            mp = op.reshape(B, -1, D_MODEL) @ w_sb_out[j]
            qs, ks, vs = sb_project(hs, w_sb_qkv[j])
            past_k = cache_k[j][page_table].reshape(n_dec, past_len, SB_HEADS, SB_HEAD_DIM)
            past_v = cache_v[j][page_table].reshape(n_dec, past_len, SB_HEADS, SB_HEAD_DIM)
            keys = jnp.concatenate([past_k, ks.astype(past_k.dtype)], axis=1)
            vals = jnp.concatenate([past_v, vs.astype(past_v.dtype)], axis=1)
            os_ = stick_breaking(qs, keys, vals, pos_s, jnp.arange(keys.shape[1]), sb_bias[j])
            ms = os_.reshape(n_dec, -1, D_MODEL) @ w_sb_out[j]
            kp_rows.append(kp)
            vp_rows.append(vp)
            ks_rows.append(ks)
            vs_rows.append(vs)
        hp = post_norm_block(hp, mp, w_ffn_in[i], w_ffn_out[i], ln_mix_g[i], ln_mix_b[i], ln_ffn_g[i], ln_ffn_b[i])
        hs = post_norm_block(hs, ms, w_ffn_in[i], w_ffn_out[i], ln_mix_g[i], ln_mix_b[i], ln_ffn_g[i], ln_ffn_b[i])

    return (hp[:, N_META:], hs,
            jnp.stack(ret_p), jnp.stack(kp_rows), jnp.stack(vp_rows),
            jnp.stack(ret_s), jnp.stack(ks_rows), jnp.stack(vs_rows))
```

```python
import functools

import jax
import jax.numpy as jnp
from jax import lax
from jax.experimental import pallas as pl
from jax.experimental.pallas import tpu as pltpu

D_MODEL = 1024
N_META = 16
RET_HEADS = 4
RET_QK_DIM = D_MODEL // RET_HEADS
RET_V_DIM = 2 * RET_QK_DIM
ROPE_BASE = 10000.0
SB_HEADS = 16
SB_HEAD_DIM = D_MODEL // SB_HEADS
PAGE_SIZE = 128
DEPTH = 4
DEEPNORM_ALPHA = (2 * DEPTH) ** 0.25
LN_EPS = 1e-5

BLK = 128
PAD_FRONT = BLK - N_META
ROW_TILE = 512
VMEM_LIMIT = 56 * 1024 * 1024

F32 = jnp.float32
BF16 = jnp.bfloat16
NT = (((1,), (1,)), ((), ()))
TN = (((0,), (0,)), ((), ()))


def _params(*sem):
    return pltpu.CompilerParams(dimension_semantics=sem, vmem_limit_bytes=VMEM_LIMIT)


def _resident(shape):
    nd = len(shape)
    return pl.BlockSpec(shape, lambda *_: (0,) * nd, pipeline_mode=pl.Buffered(1))


def _layer_norm(y, g, b):
    mu = jnp.mean(y, axis=-1, keepdims=True)
    d = y - mu
    var = jnp.mean(d * d, axis=-1, keepdims=True)
    return d * lax.rsqrt(var + LN_EPS) * g + b


def _silu(x):
    return x * jax.nn.sigmoid(x)


def _proj_kernel(x_ref, w_ref, *o_refs, col_chunk):
    xb = x_ref[...].astype(BF16)
    for c in range(w_ref.shape[1] // col_chunk):
        cols = slice(c * col_chunk, (c + 1) * col_chunk)
        acc = jnp.dot(xb, w_ref[:, cols], preferred_element_type=F32)
        for o_ref in o_refs:
            o_ref[:, cols] = acc.astype(o_ref.dtype)


def _proj(x, w, out_dtypes, name):
    rows, k = x.shape
    n = w.shape[1]
    return pl.pallas_call(
        functools.partial(_proj_kernel, col_chunk=512),
        grid=(rows // ROW_TILE,),
        in_specs=[pl.BlockSpec((ROW_TILE, k), lambda i: (i, 0)), _resident((k, n))],
        out_specs=[pl.BlockSpec((ROW_TILE, n), lambda i: (i, 0)) for _ in out_dtypes],
        out_shape=[jax.ShapeDtypeStruct((rows, n), dt) for dt in out_dtypes],
        compiler_params=_params("parallel"),
        name=name,
    )(x, w)


def _proj_ln_kernel(x_ref, w_ref, h_ref, g_ref, b_ref, o_ref):
    m = jnp.dot(x_ref[...], w_ref[...], preferred_element_type=F32)
    o_ref[...] = _layer_norm(DEEPNORM_ALPHA * h_ref[...] + m, g_ref[...], b_ref[...])


def _proj_ln(x, w, h, g, b, name):
    rows, k = x.shape
    d = w.shape[1]
    row_spec = pl.BlockSpec((ROW_TILE, d), lambda i: (i, 0))
    return pl.pallas_call(
        _proj_ln_kernel,
        grid=(rows // ROW_TILE,),
        in_specs=[pl.BlockSpec((ROW_TILE, k), lambda i: (i, 0)), _resident((k, d)), row_spec,
                  _resident((1, d)), _resident((1, d))],
        out_specs=row_spec,
        out_shape=jax.ShapeDtypeStruct((rows, d), F32),
        compiler_params=_params("parallel"),
        name=name,
    )(x, w, h, g.reshape(1, d), b.reshape(1, d))


def _ffn_ln_kernel(x_ref, wi_ref, wo_ref, g_ref, b_ref, o_ref, acc_ref, *, hid_chunk):
    x = x_ref[...]
    xb = x.astype(BF16)
    hidden = wo_ref.shape[0]
    for c in range(hidden // hid_chunk):
        lo = c * hid_chunk
        a = jnp.dot(xb, wi_ref[:, lo:lo + hid_chunk], preferred_element_type=F32)
        b = jnp.dot(xb, wi_ref[:, hidden + lo:hidden + lo + hid_chunk], preferred_element_type=F32)
        part = jnp.dot((_silu(a) * b).astype(BF16), wo_ref[lo:lo + hid_chunk, :], preferred_element_type=F32)
        if c == 0:
            acc_ref[...] = part
        else:
            acc_ref[...] += part
    o_ref[...] = _layer_norm(DEEPNORM_ALPHA * x + acc_ref[...], g_ref[...], b_ref[...])


def _ffn_ln(x, w_in, w_out, g, b, name):
    rows, d = x.shape
    hidden = w_out.shape[0]
    row_spec = pl.BlockSpec((ROW_TILE, d), lambda i: (i, 0))
    return pl.pallas_call(
        functools.partial(_ffn_ln_kernel, hid_chunk=256),
        grid=(rows // ROW_TILE,),
        in_specs=[row_spec, _resident((d, 2 * hidden)), _resident((hidden, d)),
                  _resident((1, d)), _resident((1, d))],
        out_specs=row_spec,
        out_shape=jax.ShapeDtypeStruct((rows, d), F32),
        scratch_shapes=[pltpu.VMEM((ROW_TILE, d), F32)],
        compiler_params=_params("parallel"),
        name=name,
    )(x, w_in, w_out, g.reshape(1, d), b.reshape(1, d))


def _rotate_lanes(x, cos, sin):
    half = x.shape[-1] // 2
    x1, x2 = x[:, :half], x[:, half:]
    return jnp.concatenate([x1 * cos - x2 * sin, x1 * sin + x2 * cos], axis=-1)


def _rms_gate(o, g):
    on = o * lax.rsqrt(jnp.mean(o * o, axis=-1, keepdims=True) + LN_EPS)
    return _silu(g) * on


def _ret_prompt_kernel(q_ref, k_ref, v_ref, g_ref, cos_ref, sin_ref, dec_ref, qd_ref, kd_ref,
                       o_ref, st_ref, state_ref):
    c = pl.program_id(2)

    @pl.when(c == 0)
    def _():
        state_ref[...] = jnp.zeros_like(state_ref)

    cos, sin = cos_ref[...], sin_ref[...]
    q = _rotate_lanes(q_ref[...].astype(F32), cos, sin)
    k = _rotate_lanes(k_ref[...].astype(F32), cos, sin) * (RET_QK_DIM ** -0.5)
    row = c * BLK + lax.broadcasted_iota(jnp.int32, (BLK, 1), 0)
    k = jnp.where(row >= PAD_FRONT, k, 0.0)
    vb = v_ref[...]
    qd, kd = qd_ref[...], kd_ref[...]
    state = state_ref[...]

    scores = lax.dot_general(q.astype(BF16), k.astype(BF16), NT, preferred_element_type=F32) * dec_ref[...]
    o = (jnp.dot(scores.astype(BF16), vb, preferred_element_type=F32)
         + jnp.dot((q * qd).astype(BF16), state.astype(BF16), preferred_element_type=F32))
    chunk_decay = qd[BLK - 1:BLK, :]
    state_ref[...] = chunk_decay * state + lax.dot_general(
        (k * kd).astype(BF16), vb, TN, preferred_element_type=F32)
    o_ref[...] = _rms_gate(o, g_ref[...].astype(F32)).astype(o_ref.dtype)

    @pl.when(c == pl.num_programs(2) - 1)
    def _():
        st_ref[...] = state_ref[...]


def _ret_prompt(qkvg, cos, sin, dec, qd, kd, batch, nblk, rows):
    h, dk, dv = RET_HEADS, RET_QK_DIM, RET_V_DIM
    v_blk0, g_blk0 = 2 * h * dk // dv, (2 * h * dk + h * dv) // dv
    return pl.pallas_call(
        _ret_prompt_kernel,
        grid=(batch, h, nblk),
        in_specs=[
            pl.BlockSpec((BLK, dk), lambda b, hh, c: (b * nblk + c, hh)),
            pl.BlockSpec((BLK, dk), lambda b, hh, c: (b * nblk + c, h + hh)),
            pl.BlockSpec((BLK, dv), lambda b, hh, c: (b * nblk + c, v_blk0 + hh)),
            pl.BlockSpec((BLK, dv), lambda b, hh, c: (b * nblk + c, g_blk0 + hh)),
            pl.BlockSpec((BLK, dk // 2), lambda b, hh, c: (c, 0)),
            pl.BlockSpec((BLK, dk // 2), lambda b, hh, c: (c, 0)),
            pl.BlockSpec((None, BLK, BLK), lambda b, hh, c: (hh, 0, 0)),
            pl.BlockSpec((None, BLK, 1), lambda b, hh, c: (hh, 0, 0)),
            pl.BlockSpec((None, BLK, 1), lambda b, hh, c: (hh, 0, 0)),
        ],
        out_specs=[
            pl.BlockSpec((BLK, dv), lambda b, hh, c: (b * nblk + c, hh)),
            pl.BlockSpec((None, None, dk, dv), lambda b, hh, c: (b, hh, 0, 0)),
        ],
        out_shape=[jax.ShapeDtypeStruct((rows, h * dv), BF16),
                   jax.ShapeDtypeStruct((batch, h, dk, dv), F32)],
        scratch_shapes=[pltpu.VMEM((dk, dv), F32)],
        compiler_params=_params("parallel", "parallel", "arbitrary"),
        name="ret_prompt",
    )(qkvg, qkvg, qkvg, qkvg, cos, sin, dec, qd, kd)


def _ret_decode_kernel(q_ref, kr_ref, kc_ref, v_ref, g_ref, cr_ref, sr_ref, cc_ref, sc_ref, gam_ref,
                       st_ref, o_ref, so_ref):
    scale = RET_QK_DIM ** -0.5
    q = _rotate_lanes(q_ref[...], cr_ref[...], sr_ref[...])
    k_row = _rotate_lanes(kr_ref[...], cr_ref[...], sr_ref[...]) * scale
    half = RET_QK_DIM // 2
    kc, cc, sc = kc_ref[...], cc_ref[...], sc_ref[...]
    k1, k2 = kc[:half], kc[half:]
    k_col = jnp.concatenate([k1 * cc - k2 * sc, k1 * sc + k2 * cc], axis=0) * scale
    gam = gam_ref[...]
    state = st_ref[...]
    v = v_ref[...]
    score = jnp.sum(q * k_row, axis=-1, keepdims=True)
    q_dec = jnp.broadcast_to(q * gam[:, :RET_QK_DIM], (8, RET_QK_DIM)).astype(BF16)
    o = score * v + jnp.dot(q_dec, state.astype(BF16), preferred_element_type=F32)[0:1]
    so_ref[...] = gam * state + k_col * v
    o_ref[...] = _rms_gate(o, g_ref[...])


def _ret_decode(dec_rows, state, cos_row, sin_row, gam):
    n = dec_rows.shape[0]
    h, dk, dv = RET_HEADS, RET_QK_DIM, RET_V_DIM
    q = dec_rows[:, :h * dk].reshape(n, 1, h * dk)
    k = dec_rows[:, h * dk:2 * h * dk]
    v = dec_rows[:, 2 * h * dk:2 * h * dk + h * dv].reshape(n, 1, h * dv)
    g = dec_rows[:, 2 * h * dk + h * dv:].reshape(n, 1, h * dv)
    row_qk = pl.BlockSpec((None, 1, dk), lambda s, hh: (s, 0, hh))
    row_v = pl.BlockSpec((None, 1, dv), lambda s, hh: (s, 0, hh))
    st_spec = pl.BlockSpec((None, None, dk, dv), lambda s, hh: (s, hh, 0, 0))
    const2 = lambda shape: pl.BlockSpec(shape, lambda s, hh: (0, 0))
    return pl.pallas_call(
        _ret_decode_kernel,
        grid=(n, h),
        in_specs=[row_qk, row_qk, pl.BlockSpec((None, None, dk, 1), lambda s, hh: (s, hh, 0, 0)),
                  row_v, row_v,
                  const2((1, dk // 2)), const2((1, dk // 2)), const2((dk // 2, 1)), const2((dk // 2, 1)),
                  pl.BlockSpec((None, 1, dv), lambda s, hh: (hh, 0, 0)), st_spec],
        out_specs=[row_v, st_spec],
        out_shape=[jax.ShapeDtypeStruct((n, 1, h * dv), F32), jax.ShapeDtypeStruct(state.shape, F32)],
        compiler_params=_params("parallel", "parallel"),
        name="ret_decode",
    )(q, k.reshape(n, 1, h * dk), k.reshape(n, h, dk, 1), v, g,
      cos_row, sin_row, cos_row.reshape(-1, 1), sin_row.reshape(-1, 1), gam, state)


def _log_keep_and_log_beta(z):
    tail = jnp.log1p(jnp.exp(-jnp.abs(z)))
    return -(jnp.maximum(z, 0.0) + tail), jnp.minimum(z, 0.0) - tail


def _sb_prompt_kernel(q_ref, k_ref, v_ref, bias_ref, tri_ref, o_ref, acc_ref, carry_ref):
    i = pl.program_id(2)
    lane = lax.broadcasted_iota(jnp.int32, (BLK, BLK), 1)
    sub = lax.broadcasted_iota(jnp.int32, (BLK, BLK), 0)
    qf = q_ref[...].astype(F32) * (SB_HEAD_DIM ** -0.5)
    q_heads = [jnp.where(lane < SB_HEAD_DIM, qf, 0.0).astype(BF16),
               jnp.where(lane >= SB_HEAD_DIM, qf, 0.0).astype(BF16)]
    tri = tri_ref[...]
    acc_ref[...] = jnp.zeros_like(acc_ref)
    carry_ref[...] = jnp.zeros_like(carry_ref)

    def process(j, masked):
        off = pl.multiple_of(j * BLK, BLK)
        kb = k_ref[pl.ds(off, BLK), :]
        vb = v_ref[pl.ds(off, BLK), :]
        if masked:
            key_pos = j * BLK + lane
            visible = (key_pos < i * BLK + sub) & (key_pos >= PAD_FRONT)
        for e in range(2):
            z = lax.dot_general(q_heads[e], kb, NT, preferred_element_type=F32) + bias_ref[e]
            log_keep, log_beta = _log_keep_and_log_beta(z)
            if masked:
                log_keep = jnp.where(visible, log_keep, 0.0)
            hi = log_keep.astype(BF16)
            lo = (log_keep - hi.astype(F32)).astype(BF16)
            sums = (jnp.dot(hi, tri, preferred_element_type=F32)
                    + jnp.dot(lo, tri, preferred_element_type=F32))
            log_later = sums[:, :BLK] + carry_ref[e]
            a = jnp.exp(log_beta + log_later)
            if masked:
                a = jnp.where(visible, a, 0.0)
            acc_ref[e] += jnp.dot(a.astype(BF16), vb, preferred_element_type=F32)
            carry_ref[e] += sums[:, BLK:]

    process(i, True)

    def body(t, carry):
        process(i - 1 - t, False)
        return carry

    lax.fori_loop(0, jnp.maximum(i - 1, 0), body, 0)

    @pl.when(i > 0)
    def _():
        process(0, True)

    o_ref[...] = jnp.where(lane < SB_HEAD_DIM, acc_ref[0], acc_ref[1]).astype(o_ref.dtype)


def _sb_prompt(qkv, bias_rows, tri, batch, nblk, rows):
    lp = nblk * BLK
    pairs = D_MODEL // BLK
    return pl.pallas_call(
        _sb_prompt_kernel,
        grid=(batch, pairs, nblk),
        in_specs=[
            pl.BlockSpec((BLK, BLK), lambda b, p, i: (b * nblk + i, p)),
            pl.BlockSpec((lp, BLK), lambda b, p, i: (b, pairs + p)),
            pl.BlockSpec((lp, BLK), lambda b, p, i: (b, 2 * pairs + p)),
            pl.BlockSpec((2, 1, BLK), lambda b, p, i: (p, 0, 0)),
            pl.BlockSpec((BLK, 2 * BLK), lambda b, p, i: (0, 0)),
        ],
        out_specs=pl.BlockSpec((BLK, BLK), lambda b, p, i: (b * nblk + i, p)),
        out_shape=jax.ShapeDtypeStruct((rows, D_MODEL), BF16),
        scratch_shapes=[pltpu.VMEM((2, BLK, BLK), F32), pltpu.VMEM((2, BLK, BLK), F32)],
        compiler_params=_params("parallel", "parallel", "arbitrary"),
        name="sb_prompt",
    )(qkv, qkv, qkv, bias_rows, tri)


def _sb_decode_kernel(pt_ref, q_ref, k_ref, v_ref, bias_ref, o_ref, acc_ref, carry_ref):
    del pt_ref
    p = pl.program_id(1)

    @pl.when(p == 0)
    def _():
        acc_ref[...] = jnp.zeros_like(acc_ref)
        carry_ref[...] = jnp.zeros_like(carry_ref)

    lane_d = lax.broadcasted_iota(jnp.int32, (SB_HEADS, D_MODEL), 1)
    head_d = lax.broadcasted_iota(jnp.int32, (SB_HEADS, D_MODEL), 0)
    own = lax.shift_right_logical(lane_d, 6) == head_d
    q = q_ref[...] * (SB_HEAD_DIM ** -0.5)
    q_heads = jnp.where(own, jnp.broadcast_to(q, (SB_HEADS, D_MODEL)), 0.0).astype(BF16)
    z = lax.dot_general(q_heads, k_ref[...].astype(BF16), NT, preferred_element_type=F32) + bias_ref[...]
    log_keep, log_beta = _log_keep_and_log_beta(z)

    lane = lax.broadcasted_iota(jnp.int32, (SB_HEADS, PAGE_SIZE), 1)
    incl = log_keep
    shift = 1
    while shift < PAGE_SIZE:
        nxt = pltpu.roll(incl, PAGE_SIZE - shift, axis=1)
        incl = incl + jnp.where(lane < PAGE_SIZE - shift, nxt, 0.0)
        shift *= 2
    a = jnp.exp(log_beta + (incl - log_keep) + carry_ref[...])
    acc_ref[...] += jnp.dot(a.astype(BF16), v_ref[...].astype(BF16), preferred_element_type=F32)
    carry_ref[...] += jnp.broadcast_to(incl[:, 0:1], carry_ref.shape)

    @pl.when(p == pl.num_programs(1) - 1)
    def _():
        o_ref[...] = jnp.sum(jnp.where(own, acc_ref[...], 0.0), axis=0, keepdims=True)


def _sb_decode(q, cache_k, cache_v, layer, page_table, bias_col):
    n, n_pages = page_table.shape
    n_pool = cache_k.shape[1]
    ck = cache_k.reshape(cache_k.shape[0], n_pool, PAGE_SIZE, D_MODEL)
    cv = cache_v.reshape(cache_v.shape[0], n_pool, PAGE_SIZE, D_MODEL)
    page_spec = pl.BlockSpec((None, None, PAGE_SIZE, D_MODEL),
                             lambda s, p, pt: (layer, pt[s, n_pages - 1 - p], 0, 0))
    row_spec = pl.BlockSpec((None, 1, D_MODEL), lambda s, p, pt: (s, 0, 0))
    return pl.pallas_call(
        _sb_decode_kernel,
        grid_spec=pltpu.PrefetchScalarGridSpec(
            num_scalar_prefetch=1,
            grid=(n, n_pages),
            in_specs=[row_spec, page_spec, page_spec,
                      pl.BlockSpec((SB_HEADS, 1), lambda s, p, pt: (0, 0))],
            out_specs=row_spec,
            scratch_shapes=[pltpu.VMEM((SB_HEADS, D_MODEL), F32), pltpu.VMEM((SB_HEADS, PAGE_SIZE), F32)],
        ),
        out_shape=jax.ShapeDtypeStruct((n, 1, D_MODEL), F32),
        compiler_params=_params("parallel", "arbitrary"),
        name="sb_decode",
    )(page_table, q, ck, cv, bias_col)


def _retention_tables(nblk):
    log_g = jnp.log(1.0 - 2.0 ** (-5.0 - jnp.arange(RET_HEADS, dtype=F32)))
    idx = jnp.arange(BLK, dtype=F32)
    diff = idx[:, None] - idx[None, :]
    lg = log_g[:, None, None]
    dec = jnp.where(diff >= 0, jnp.exp(lg * jnp.maximum(diff, 0.0)), 0.0)
    qd = jnp.exp(log_g[:, None] * (idx[None, :] + 1.0))[:, :, None]
    kd = jnp.exp(log_g[:, None] * (BLK - 1.0 - idx[None, :]))[:, :, None]
    gam = jnp.broadcast_to(jnp.exp(log_g)[:, None, None], (RET_HEADS, 1, RET_V_DIM))
    return dec, qd, kd, gam


def _rope_tables(pos):
    half = RET_QK_DIM // 2
    inv = ROPE_BASE ** (-jnp.arange(half, dtype=F32) / half)
    ang = pos.astype(F32)[:, None] * inv[None, :]
    return jnp.cos(ang), jnp.sin(ang)


def _with_decode_rows(prompt_rows_out, dec_rows, rows_prompt):
    n, width = dec_rows.shape
    tile = jnp.concatenate([dec_rows, jnp.zeros((ROW_TILE - n, width), dec_rows.dtype)], axis=0)
    return lax.dynamic_update_slice(prompt_rows_out, tile.astype(prompt_rows_out.dtype), (rows_prompt, 0))


def kernel(x_prompt, x_sample, state_ret, cache_k, cache_v, page_table, meta_tokens, w_ret_in, w_ret_out,
           w_sb_qkv, w_sb_out, sb_bias, w_ffn_in, w_ffn_out, ln_mix_g, ln_mix_b, ln_ffn_g, ln_ffn_b):
    batch, seq, d = x_prompt.shape
    n_dec = x_sample.shape[0]
    assert d == D_MODEL and x_sample.shape[1] == 1 and n_dec <= ROW_TILE
    assert (N_META + seq) % BLK == N_META and cache_k.shape[2] == PAGE_SIZE
    nblk = (PAD_FRONT + N_META + seq) // BLK
    lp = nblk * BLK
    rows_prompt = batch * lp
    assert rows_prompt % ROW_TILE == 0
    rows = rows_prompt + ROW_TILE
    past_len = page_table.shape[1] * PAGE_SIZE

    w_ret_in, w_ret_out, w_sb_qkv, w_sb_out, w_ffn_in, w_ffn_out = (
        w.astype(BF16) for w in (w_ret_in, w_ret_out, w_sb_qkv, w_sb_out, w_ffn_in, w_ffn_out))

    hp = jnp.concatenate([jnp.zeros((batch, PAD_FRONT, d), F32),
                          jnp.broadcast_to(meta_tokens[None].astype(F32), (batch, N_META, d)),
                          x_prompt], axis=1).reshape(rows_prompt, d)
    h = jnp.concatenate([hp, x_sample.reshape(n_dec, d), jnp.zeros((ROW_TILE - n_dec, d), F32)], axis=0)

    dec, qd, kd, gam = _retention_tables(nblk)
    cos_p, sin_p = _rope_tables(jnp.arange(lp) - PAD_FRONT)
    cos_s, sin_s = _rope_tables(past_len + jnp.arange(1))
    idx = jnp.arange(BLK)
    tri = jnp.concatenate([(idx[:, None] > idx[None, :]), jnp.ones((BLK, BLK), bool)], axis=1).astype(BF16)

    ret_p, ret_s, kp_rows, vp_rows, ks_rows, vs_rows = [], [], [], [], [], []
    for i in range(DEPTH):
        j = i // 2
        if i % 2 == 0:
            (qkvg,) = _proj(h, w_ret_in[j], [BF16], f"ret_in_{j}")
            mix_p, st_p = _ret_prompt(qkvg, cos_p, sin_p, dec, qd, kd, batch, nblk, rows)
            mix_s, st_s = _ret_decode(qkvg[rows_prompt:rows_prompt + n_dec].astype(F32), state_ret[j],
                                      cos_s, sin_s, gam)
            ret_p.append(st_p)
            ret_s.append(st_s)
            mix = _with_decode_rows(mix_p, mix_s.reshape(n_dec, -1), rows_prompt)
            w_out = w_ret_out[j]
        else:
            qkv, qkv_b = _proj(h, w_sb_qkv[j], [F32, BF16], f"sb_qkv_{j}")
            bias = sb_bias[j].astype(F32)
            mix_p = _sb_prompt(qkv_b, jnp.broadcast_to(bias[:, None, None], (SB_HEADS, 1, BLK)), tri,
                               batch, nblk, rows)
            dec_rows = qkv[rows_prompt:rows_prompt + n_dec]
            mix_s = _sb_decode(dec_rows[:, :d].reshape(n_dec, 1, d), cache_k, cache_v, j, page_table,
                               bias.reshape(SB_HEADS, 1))
            mix = _with_decode_rows(mix_p, mix_s.reshape(n_dec, d), rows_prompt)
            w_out = w_sb_out[j]
            to_rows = lambda t: t.reshape(batch, lp, SB_HEADS, SB_HEAD_DIM)[:, PAD_FRONT:]
            kp_rows.append(to_rows(qkv[:rows_prompt, d:2 * d]))
            vp_rows.append(to_rows(qkv[:rows_prompt, 2 * d:]))
            ks_rows.append(dec_rows[:, d:2 * d].reshape(n_dec, 1, SB_HEADS, SB_HEAD_DIM))
            vs_rows.append(dec_rows[:, 2 * d:].reshape(n_dec, 1, SB_HEADS, SB_HEAD_DIM))
        h = _proj_ln(mix, w_out, h, ln_mix_g[i], ln_mix_b[i], f"mix_out_ln_{i}")
        h = _ffn_ln(h, w_ffn_in[i], w_ffn_out[i], ln_ffn_g[i], ln_ffn_b[i], f"ffn_ln_{i}")

    y_prompt = h[:rows_prompt].reshape(batch, lp, d)[:, BLK:]
    y_sample = h[rows_prompt:rows_prompt + n_dec].reshape(n_dec, 1, d)
    return (y_prompt, y_sample, jnp.stack(ret_p), jnp.stack(kp_rows), jnp.stack(vp_rows),
            jnp.stack(ret_s), jnp.stack(ks_rows), jnp.stack(vs_rows))
```

```python
import functools

import jax
import jax.numpy as jnp
from jax import lax
from jax.experimental import pallas as pl
from jax.experimental.pallas import tpu as pltpu

D_MODEL = 1024
N_META = 16
RET_HEADS = 4
RET_QK_DIM = D_MODEL // RET_HEADS
RET_V_DIM = 2 * RET_QK_DIM
ROPE_BASE = 10000.0
SB_HEADS = 16
SB_HEAD_DIM = D_MODEL // SB_HEADS
PAGE_SIZE = 128
DEPTH = 4
DEEPNORM_ALPHA = (2 * DEPTH) ** 0.25
LN_EPS = 1e-5

BLK = 128
ROW_TILE = 512
SB_CHUNK_BLOCKS = 4
DEC_PAGES_PER_STEP = 8
VMEM_LIMIT = 56 * 1024 * 1024

F32 = jnp.float32
BF16 = jnp.bfloat16
NT = (((1,), (1,)), ((), ()))
TN = (((0,), (0,)), ((), ()))


def _params(*sem):
    return pltpu.CompilerParams(dimension_semantics=sem, vmem_limit_bytes=VMEM_LIMIT)


def _resident(shape):
    nd = len(shape)
    return pl.BlockSpec(shape, lambda *_: (0,) * nd, pipeline_mode=pl.Buffered(1))


def _layer_norm(y, g, b):
    mu = jnp.mean(y, axis=-1, keepdims=True)
    d = y - mu
    var = jnp.mean(d * d, axis=-1, keepdims=True)
    return d * lax.rsqrt(var + LN_EPS) * g + b


def _silu(x):
    return x * jax.nn.sigmoid(x)


def _proj_kernel(x_ref, w_ref, *o_refs, col_chunk):
    xb = x_ref[...].astype(BF16)
    for c in range(w_ref.shape[1] // col_chunk):
        cols = slice(c * col_chunk, (c + 1) * col_chunk)
        acc = jnp.dot(xb, w_ref[:, cols], preferred_element_type=F32)
        for o_ref in o_refs:
            o_ref[:, cols] = acc.astype(o_ref.dtype)


def _proj(x, w, out_dtypes, name):
    rows, k = x.shape
    n = w.shape[1]
    return pl.pallas_call(
        functools.partial(_proj_kernel, col_chunk=512),
        grid=(rows // ROW_TILE,),
        in_specs=[pl.BlockSpec((ROW_TILE, k), lambda i: (i, 0)), _resident((k, n))],
        out_specs=[pl.BlockSpec((ROW_TILE, n), lambda i: (i, 0)) for _ in out_dtypes],
        out_shape=[jax.ShapeDtypeStruct((rows, n), dt) for dt in out_dtypes],
        compiler_params=_params("parallel"),
        name=name,
    )(x, w)


def _proj_ln_kernel(x_ref, w_ref, h_ref, g_ref, b_ref, o_ref):
    m = jnp.dot(x_ref[...], w_ref[...], preferred_element_type=F32)
    o_ref[...] = _layer_norm(DEEPNORM_ALPHA * h_ref[...] + m, g_ref[...], b_ref[...])


def _proj_ln(x, w, h, g, b, name):
    rows, k = x.shape
    d = w.shape[1]
    row_spec = pl.BlockSpec((ROW_TILE, d), lambda i: (i, 0))
    return pl.pallas_call(
        _proj_ln_kernel,
        grid=(rows // ROW_TILE,),
        in_specs=[pl.BlockSpec((ROW_TILE, k), lambda i: (i, 0)), _resident((k, d)), row_spec,
                  _resident((1, d)), _resident((1, d))],
        out_specs=row_spec,
        out_shape=jax.ShapeDtypeStruct((rows, d), F32),
        compiler_params=_params("parallel"),
        name=name,
    )(x, w, h, g.reshape(1, d), b.reshape(1, d))


def _ffn_ln_kernel(x_ref, wi_ref, wo_ref, g_ref, b_ref, o_ref, acc_ref, *, hid_chunk):
    x = x_ref[...]
    xb = x.astype(BF16)
    hidden = wo_ref.shape[0]
    for c in range(hidden // hid_chunk):
        lo = c * hid_chunk
        a = jnp.dot(xb, wi_ref[:, lo:lo + hid_chunk], preferred_element_type=F32)
        b = jnp.dot(xb, wi_ref[:, hidden + lo:hidden + lo + hid_chunk], preferred_element_type=F32)
        part = jnp.dot((_silu(a) * b).astype(BF16), wo_ref[lo:lo + hid_chunk, :], preferred_element_type=F32)
        if c == 0:
            acc_ref[...] = part
        else:
            acc_ref[...] += part
    o_ref[...] = _layer_norm(DEEPNORM_ALPHA * x + acc_ref[...], g_ref[...], b_ref[...])


def _ffn_ln(x, w_in, w_out, g, b, name):
    rows, d = x.shape
    hidden = w_out.shape[0]
    row_spec = pl.BlockSpec((ROW_TILE, d), lambda i: (i, 0))
    return pl.pallas_call(
        functools.partial(_ffn_ln_kernel, hid_chunk=256),
        grid=(rows // ROW_TILE,),
        in_specs=[row_spec, _resident((d, 2 * hidden)), _resident((hidden, d)),
                  _resident((1, d)), _resident((1, d))],
        out_specs=row_spec,
        out_shape=jax.ShapeDtypeStruct((rows, d), F32),
        scratch_shapes=[pltpu.VMEM((ROW_TILE, d), F32)],
        compiler_params=_params("parallel"),
        name=name,
    )(x, w_in, w_out, g.reshape(1, d), b.reshape(1, d))


def _rotate_lanes(x, cos, sin):
    half = x.shape[-1] // 2
    x1, x2 = x[:, :half], x[:, half:]
    return jnp.concatenate([x1 * cos - x2 * sin, x1 * sin + x2 * cos], axis=-1)


def _rms_gate(o, g):
    on = o * lax.rsqrt(jnp.mean(o * o, axis=-1, keepdims=True) + LN_EPS)
    return _silu(g) * on


def _ret_prompt_kernel(q_ref, k_ref, v_ref, g_ref, cos_ref, sin_ref, dec_ref, qd_ref, kd_ref, cd_ref,
                       o_ref, st_ref, state_ref, *, n_real):
    c = pl.program_id(2)

    @pl.when(c == 0)
    def _():
        state_ref[...] = jnp.zeros_like(state_ref)

    cos, sin = cos_ref[...], sin_ref[...]
    q = _rotate_lanes(q_ref[...].astype(F32), cos, sin)
    k = _rotate_lanes(k_ref[...].astype(F32), cos, sin) * (RET_QK_DIM ** -0.5)
    row = c * BLK + lax.broadcasted_iota(jnp.int32, (BLK, 1), 0)
    k = jnp.where(row < n_real, k, 0.0)
    vb = v_ref[...]
    state = state_ref[...]

    scores = lax.dot_general(q.astype(BF16), k.astype(BF16), NT, preferred_element_type=F32) * dec_ref[...]
    o = (jnp.dot(scores.astype(BF16), vb, preferred_element_type=F32)
         + jnp.dot((q * qd_ref[...]).astype(BF16), state.astype(BF16), preferred_element_type=F32))
    state_ref[...] = cd_ref[:, 0:1] * state + lax.dot_general(
        (k * kd_ref[...]).astype(BF16), vb, TN, preferred_element_type=F32)
    o_ref[...] = _rms_gate(o, g_ref[...].astype(F32)).astype(o_ref.dtype)

    @pl.when(c == pl.num_programs(2) - 1)
    def _():
        st_ref[...] = state_ref[...]


def _ret_prompt(qkvg, cos, sin, dec, qd, kd, cd, batch, nblk, rows, n_real):
    h, dk, dv = RET_HEADS, RET_QK_DIM, RET_V_DIM
    v_blk0, g_blk0 = 2 * h * dk // dv, (2 * h * dk + h * dv) // dv
    last = lambda c: (c == nblk - 1).astype(jnp.int32)
    return pl.pallas_call(
        functools.partial(_ret_prompt_kernel, n_real=n_real),
        grid=(batch, h, nblk),
        in_specs=[
            pl.BlockSpec((BLK, dk), lambda b, hh, c: (b * nblk + c, hh)),
            pl.BlockSpec((BLK, dk), lambda b, hh, c: (b * nblk + c, h + hh)),
            pl.BlockSpec((BLK, dv), lambda b, hh, c: (b * nblk + c, v_blk0 + hh)),
            pl.BlockSpec((BLK, dv), lambda b, hh, c: (b * nblk + c, g_blk0 + hh)),
            pl.BlockSpec((BLK, dk // 2), lambda b, hh, c: (c, 0)),
            pl.BlockSpec((BLK, dk // 2), lambda b, hh, c: (c, 0)),
            pl.BlockSpec((None, BLK, BLK), lambda b, hh, c: (hh, 0, 0)),
            pl.BlockSpec((None, BLK, 1), lambda b, hh, c: (hh, 0, 0)),
            pl.BlockSpec((None, None, BLK, 1), lambda b, hh, c: (last(c), hh, 0, 0)),
            pl.BlockSpec((None, None, 1, BLK), lambda b, hh, c: (last(c), hh, 0, 0)),
        ],
        out_specs=[
            pl.BlockSpec((BLK, dv), lambda b, hh, c: (b * nblk + c, hh)),
            pl.BlockSpec((None, None, dk, dv), lambda b, hh, c: (b, hh, 0, 0)),
        ],
        out_shape=[jax.ShapeDtypeStruct((rows, h * dv), BF16),
                   jax.ShapeDtypeStruct((batch, h, dk, dv), F32)],
        scratch_shapes=[pltpu.VMEM((dk, dv), F32)],
        compiler_params=_params("parallel", "parallel", "arbitrary"),
        name="ret_prompt",
    )(qkvg, qkvg, qkvg, qkvg, cos, sin, dec, qd, kd, cd)


def _ret_decode_kernel(q_ref, kr_ref, kc_ref, v_ref, g_ref, cr_ref, sr_ref, cc_ref, sc_ref, gam_ref,
                       st_ref, o_ref, so_ref):
    scale = RET_QK_DIM ** -0.5
    q = _rotate_lanes(q_ref[...], cr_ref[...], sr_ref[...])
    k_row = _rotate_lanes(kr_ref[...], cr_ref[...], sr_ref[...]) * scale
    half = RET_QK_DIM // 2
    kc, cc, sc = kc_ref[...], cc_ref[...], sc_ref[...]
    k1, k2 = kc[:half], kc[half:]
    k_col = jnp.concatenate([k1 * cc - k2 * sc, k1 * sc + k2 * cc], axis=0) * scale
    gam = gam_ref[...]
    state = st_ref[...]
    v = v_ref[...]
    score = jnp.sum(q * k_row, axis=-1, keepdims=True)
    q_dec = jnp.broadcast_to(q * gam[:, :RET_QK_DIM], (8, RET_QK_DIM)).astype(BF16)
    o = score * v + jnp.dot(q_dec, state.astype(BF16), preferred_element_type=F32)[0:1]
    so_ref[...] = gam * state + k_col * v
    o_ref[...] = _rms_gate(o, g_ref[...])


def _ret_decode(dec_rows, state, cos_row, sin_row, gam):
    n = dec_rows.shape[0]
    h, dk, dv = RET_HEADS, RET_QK_DIM, RET_V_DIM
    q = dec_rows[:, :h * dk].reshape(n, 1, h * dk)
    k = dec_rows[:, h * dk:2 * h * dk]
    v = dec_rows[:, 2 * h * dk:2 * h * dk + h * dv].reshape(n, 1, h * dv)
    g = dec_rows[:, 2 * h * dk + h * dv:].reshape(n, 1, h * dv)
    row_qk = pl.BlockSpec((None, 1, dk), lambda s, hh: (s, 0, hh))
    row_v = pl.BlockSpec((None, 1, dv), lambda s, hh: (s, 0, hh))
    st_spec = pl.BlockSpec((None, None, dk, dv), lambda s, hh: (s, hh, 0, 0))
    const2 = lambda shape: pl.BlockSpec(shape, lambda s, hh: (0, 0))
    return pl.pallas_call(
        _ret_decode_kernel,
        grid=(n, h),
        in_specs=[row_qk, row_qk, pl.BlockSpec((None, None, dk, 1), lambda s, hh: (s, hh, 0, 0)),
                  row_v, row_v,
                  const2((1, dk // 2)), const2((1, dk // 2)), const2((dk // 2, 1)), const2((dk // 2, 1)),
                  pl.BlockSpec((None, 1, dv), lambda s, hh: (hh, 0, 0)), st_spec],
        out_specs=[row_v, st_spec],
        out_shape=[jax.ShapeDtypeStruct((n, 1, h * dv), F32), jax.ShapeDtypeStruct(state.shape, F32)],
        compiler_params=_params("parallel", "parallel"),
        name="ret_decode",
    )(q, k.reshape(n, 1, h * dk), k.reshape(n, h, dk, 1), v, g,
      cos_row, sin_row, cos_row.reshape(-1, 1), sin_row.reshape(-1, 1), gam, state)


def _softplus(z):
    return jnp.maximum(z, 0.0) + jnp.log(1.0 + jnp.exp(-jnp.abs(z)))


def _suffix_sums(keep, tri2):
    hi = keep.astype(BF16)
    lo = (keep - hi.astype(F32)).astype(BF16)
    return jnp.dot(jnp.concatenate([hi, lo], axis=1), tri2, preferred_element_type=F32)


def _sb_prompt_kernel(q_ref, k_ref, v_ref, bias_ref, tri2_ref, o_ref, qh_ref, acc_ref, carry_ref, *, chunks):
    step = pl.program_id(2)
    nblk = pl.num_programs(2)
    j = nblk - 1 - step

    @pl.when(step == 0)
    def _():
        lane = lax.broadcasted_iota(jnp.int32, q_ref.shape, 1)
        qf = q_ref[...].astype(F32) * (SB_HEAD_DIM ** -0.5)
        qh_ref[0] = jnp.where(lane < SB_HEAD_DIM, qf, 0.0).astype(BF16)
        qh_ref[1] = jnp.where(lane >= SB_HEAD_DIM, qf, 0.0).astype(BF16)
        acc_ref[...] = jnp.zeros_like(acc_ref)
        carry_ref[...] = jnp.zeros_like(carry_ref)

    def chunk(first_blk, n_blk, masked):
        r0, nr = first_blk * BLK, n_blk * BLK
        rows = slice(r0, r0 + nr)
        lane = lax.broadcasted_iota(jnp.int32, (nr, BLK), 1)
        q2 = jnp.concatenate([qh_ref[0, rows, :], qh_ref[1, rows, :]], axis=0)
        bias2 = jnp.concatenate([jnp.broadcast_to(bias_ref[0], (nr, BLK)),
                                 jnp.broadcast_to(bias_ref[1], (nr, BLK))], axis=0)
        z = lax.dot_general(q2, k_ref[...], NT, preferred_element_type=F32) + bias2
        keep = _softplus(z)
        if masked:
            earlier = (lane - lax.broadcasted_iota(jnp.int32, (nr, BLK), 0)) < (r0 - j * BLK)
            visible = jnp.concatenate([earlier, earlier], axis=0)
            keep = jnp.where(visible, keep, 0.0)
        sums = _suffix_sums(keep, tri2_ref[...])
        carry = jnp.concatenate([carry_ref[0, rows, :], carry_ref[1, rows, :]], axis=0)
        a = jnp.exp(z - sums[:, :BLK] - carry)
        if masked:
            a = jnp.where(visible, a, 0.0)
        pv = jnp.dot(a.astype(BF16), v_ref[...], preferred_element_type=F32)
        new_acc = acc_ref[rows, :] + jnp.where(lane < SB_HEAD_DIM, pv[:nr], pv[nr:])
        return rows, nr, new_acc, carry + sums[:, BLK:]

    for c_lo, (first_blk, n_blk) in enumerate(chunks):
        @pl.when((j >= first_blk) & (j < first_blk + n_blk))
        def _(c_lo=c_lo):
            results = [chunk(*chunks[c], masked=(c == c_lo)) for c in range(c_lo, len(chunks))]
            for rows, nr, new_acc, new_carry in results:
                acc_ref[rows, :] = new_acc
                carry_ref[0, rows, :] = new_carry[:nr]
                carry_ref[1, rows, :] = new_carry[nr:]

    @pl.when(step == nblk - 1)
    def _():
        o_ref[...] = acc_ref[...].astype(o_ref.dtype)


def _sb_chunks(nblk):
    n = max(nblk // SB_CHUNK_BLOCKS, 1)
    return tuple((c * SB_CHUNK_BLOCKS, SB_CHUNK_BLOCKS if c < n - 1 else nblk - c * SB_CHUNK_BLOCKS)
                 for c in range(n))


def _sb_prompt(qkv, bias_rows, tri2, batch, nblk, rows):
    lp = nblk * BLK
    pairs = D_MODEL // BLK
    return pl.pallas_call(
        functools.partial(_sb_prompt_kernel, chunks=_sb_chunks(nblk)),
        grid=(batch, pairs, nblk),
        in_specs=[
            pl.BlockSpec((lp, BLK), lambda b, p, s: (b, p)),
            pl.BlockSpec((BLK, BLK), lambda b, p, s: (b * nblk + nblk - 1 - s, pairs + p)),
            pl.BlockSpec((BLK, BLK), lambda b, p, s: (b * nblk + nblk - 1 - s, 2 * pairs + p)),
            pl.BlockSpec((2, 1, BLK), lambda b, p, s: (p, 0, 0)),
            pl.BlockSpec((2 * BLK, 2 * BLK), lambda b, p, s: (0, 0)),
        ],
        out_specs=pl.BlockSpec((lp, BLK), lambda b, p, s: (b, p)),
        out_shape=jax.ShapeDtypeStruct((rows, D_MODEL), BF16),
        scratch_shapes=[pltpu.VMEM((2, lp, BLK), BF16), pltpu.VMEM((lp, BLK), F32),
                        pltpu.VMEM((2, lp, BLK), F32)],
        compiler_params=_params("parallel", "parallel", "arbitrary"),
        name="sb_prompt",
    )(qkv, qkv, qkv, bias_rows, tri2)


def _sb_decode_kernel(pt_ref, q_ref, *refs, n_pg):
    del pt_ref
    k_refs, v_refs = refs[:n_pg], refs[n_pg:2 * n_pg]
    bias_ref, tri2_ref, o_ref, acc_ref, carry_ref = refs[2 * n_pg:]
    g = pl.program_id(1)

    @pl.when(g == 0)
    def _():
        acc_ref[...] = jnp.zeros_like(acc_ref)
        carry_ref[...] = jnp.zeros_like(carry_ref)

    lane_d = lax.broadcasted_iota(jnp.int32, (SB_HEADS, D_MODEL), 1)
    head_d = lax.broadcasted_iota(jnp.int32, (SB_HEADS, D_MODEL), 0)
    own = lax.shift_right_logical(lane_d, 6) == head_d
    q = q_ref[...] * (SB_HEAD_DIM ** -0.5)
    q_heads = jnp.where(own, jnp.broadcast_to(q, (SB_HEADS, D_MODEL)), 0.0).astype(BF16)

    z = jnp.concatenate(
        [jnp.dot(q_heads, k_refs[p][...].reshape(D_MODEL, PAGE_SIZE).astype(BF16), preferred_element_type=F32)
         for p in range(n_pg)], axis=0) + jnp.concatenate([bias_ref[...]] * n_pg, axis=0)
    sums = _suffix_sums(_softplus(z), tri2_ref[...])
    carry = carry_ref[...]
    acc = acc_ref[...]
    for p in range(n_pg):
        rows = slice(p * SB_HEADS, (p + 1) * SB_HEADS)
        a = jnp.exp(z[rows] - sums[rows, :PAGE_SIZE] - carry)
        acc = acc + lax.dot_general(a.astype(BF16), v_refs[p][...].reshape(D_MODEL, PAGE_SIZE).astype(BF16),
                                    NT, preferred_element_type=F32)
        carry = carry + sums[rows, PAGE_SIZE:]
    acc_ref[...] = acc
    carry_ref[...] = carry

    @pl.when(g == pl.num_programs(1) - 1)
    def _():
        o_ref[...] = jnp.sum(jnp.where(own, acc, 0.0), axis=0, keepdims=True)


def _sb_decode(q, cache_k, cache_v, layer, page_table, bias_col, tri2):
    n, n_pages = page_table.shape
    n_pg = DEC_PAGES_PER_STEP
    assert n_pages % n_pg == 0
    ck = jnp.transpose(cache_k, (0, 1, 3, 4, 2))
    cv = jnp.transpose(cache_v, (0, 1, 3, 4, 2))

    def page_spec(p):
        return pl.BlockSpec((None, None, SB_HEADS, SB_HEAD_DIM, PAGE_SIZE),
                            lambda s, g, pt: (layer, pt[s, n_pages - 1 - (g * n_pg + p)], 0, 0, 0))

    row_spec = pl.BlockSpec((None, 1, D_MODEL), lambda s, g, pt: (s, 0, 0))
    return pl.pallas_call(
        functools.partial(_sb_decode_kernel, n_pg=n_pg),
        grid_spec=pltpu.PrefetchScalarGridSpec(
            num_scalar_prefetch=1,
            grid=(n, n_pages // n_pg),
            in_specs=[row_spec] + [page_spec(p) for p in range(n_pg)] * 2
                     + [pl.BlockSpec((SB_HEADS, 1), lambda s, g, pt: (0, 0)),
                        pl.BlockSpec((2 * BLK, 2 * BLK), lambda s, g, pt: (0, 0))],
            out_specs=row_spec,
            scratch_shapes=[pltpu.VMEM((SB_HEADS, D_MODEL), F32), pltpu.VMEM((SB_HEADS, PAGE_SIZE), F32)],
        ),
        out_shape=jax.ShapeDtypeStruct((n, 1, D_MODEL), F32),
        compiler_params=_params("parallel", "arbitrary"),
        name="sb_decode",
    )(page_table, q, *([ck] * n_pg), *([cv] * n_pg), bias_col, tri2)


def _retention_tables(n_last):
    log_g = jnp.log(1.0 - 2.0 ** (-5.0 - jnp.arange(RET_HEADS, dtype=F32)))
    idx = jnp.arange(BLK, dtype=F32)
    diff = idx[:, None] - idx[None, :]
    lg = log_g[:, None, None]
    dec = jnp.where(diff >= 0, jnp.exp(lg * jnp.maximum(diff, 0.0)), 0.0)
    qd = jnp.exp(log_g[:, None] * (idx[None, :] + 1.0))[:, :, None]
    n_tok = jnp.array([BLK, n_last], F32)[:, None, None]
    kd = jnp.exp(log_g[None, :, None] * (n_tok - 1.0 - idx[None, None, :]))[..., None]
    cd = jnp.broadcast_to(jnp.exp(log_g[None, :, None] * n_tok)[..., None], (2, RET_HEADS, 1, BLK))
    gam = jnp.broadcast_to(jnp.exp(log_g)[:, None, None], (RET_HEADS, 1, RET_V_DIM))
    return dec, qd, kd, cd, gam


def _rope_tables(pos):
    half = RET_QK_DIM // 2
    inv = ROPE_BASE ** (-jnp.arange(half, dtype=F32) / half)
    ang = pos.astype(F32)[:, None] * inv[None, :]
    return jnp.cos(ang), jnp.sin(ang)


def _with_decode_rows(prompt_rows_out, dec_rows, rows_prompt):
    n, width = dec_rows.shape
    tile = jnp.concatenate([dec_rows, jnp.zeros((ROW_TILE - n, width), dec_rows.dtype)], axis=0)
    return lax.dynamic_update_slice(prompt_rows_out, tile.astype(prompt_rows_out.dtype), (rows_prompt, 0))


def kernel(x_prompt, x_sample, state_ret, cache_k, cache_v, page_table, meta_tokens, w_ret_in, w_ret_out,
           w_sb_qkv, w_sb_out, sb_bias, w_ffn_in, w_ffn_out, ln_mix_g, ln_mix_b, ln_ffn_g, ln_ffn_b):
    batch, seq, d = x_prompt.shape
    n_dec = x_sample.shape[0]
    assert d == D_MODEL and x_sample.shape[1] == 1 and n_dec <= ROW_TILE
    assert cache_k.shape[2] == PAGE_SIZE
    n_real = N_META + seq
    nblk = pl.cdiv(n_real, BLK)
    lp = nblk * BLK
    rows_prompt = batch * lp
    assert rows_prompt % ROW_TILE == 0
    rows = rows_prompt + ROW_TILE
    past_len = page_table.shape[1] * PAGE_SIZE

    w_ret_in, w_ret_out, w_sb_qkv, w_sb_out, w_ffn_in, w_ffn_out = (
        w.astype(BF16) for w in (w_ret_in, w_ret_out, w_sb_qkv, w_sb_out, w_ffn_in, w_ffn_out))

    hp = jnp.concatenate([jnp.broadcast_to(meta_tokens[None].astype(F32), (batch, N_META, d)), x_prompt,
                          jnp.zeros((batch, lp - n_real, d), F32)], axis=1).reshape(rows_prompt, d)
    h = jnp.concatenate([hp, x_sample.reshape(n_dec, d), jnp.zeros((ROW_TILE - n_dec, d), F32)], axis=0)

    dec, qd, kd, cd, gam = _retention_tables(n_real - (nblk - 1) * BLK)
    cos_p, sin_p = _rope_tables(jnp.arange(lp))
    cos_s, sin_s = _rope_tables(past_len + jnp.arange(1))
    idx = jnp.arange(BLK)
    tri = jnp.concatenate([idx[:, None] >= idx[None, :], jnp.ones((BLK, BLK), bool)], axis=1)
    tri2 = jnp.concatenate([tri, tri], axis=0).astype(BF16)

    ret_p, ret_s, kp_rows, vp_rows, ks_rows, vs_rows = [], [], [], [], [], []
    for i in range(DEPTH):
        j = i // 2
        if i % 2 == 0:
            (qkvg,) = _proj(h, w_ret_in[j], [BF16], f"ret_in_{j}")
            mix_p, st_p = _ret_prompt(qkvg, cos_p, sin_p, dec, qd, kd, cd, batch, nblk, rows, n_real)
            mix_s, st_s = _ret_decode(qkvg[rows_prompt:rows_prompt + n_dec].astype(F32), state_ret[j],
                                      cos_s, sin_s, gam)
            ret_p.append(st_p)
            ret_s.append(st_s)
            mix = _with_decode_rows(mix_p, mix_s.reshape(n_dec, -1), rows_prompt)
            w_out = w_ret_out[j]
        else:
            qkv, qkv_b = _proj(h, w_sb_qkv[j], [F32, BF16], f"sb_qkv_{j}")
            bias = sb_bias[j].astype(F32)
            mix_p = _sb_prompt(qkv_b, jnp.broadcast_to(bias[:, None, None], (SB_HEADS, 1, BLK)), tri2,
                               batch, nblk, rows)
            dec_rows = qkv[rows_prompt:rows_prompt + n_dec]
            mix_s = _sb_decode(dec_rows[:, :d].reshape(n_dec, 1, d), cache_k, cache_v, j, page_table,
                               bias.reshape(SB_HEADS, 1), tri2)
            mix = _with_decode_rows(mix_p, mix_s.reshape(n_dec, d), rows_prompt)
            w_out = w_sb_out[j]
            to_rows = lambda t: t.reshape(batch, lp, SB_HEADS, SB_HEAD_DIM)[:, :n_real]
            kp_rows.append(to_rows(qkv[:rows_prompt, d:2 * d]))
            vp_rows.append(to_rows(qkv[:rows_prompt, 2 * d:]))
            ks_rows.append(dec_rows[:, d:2 * d].reshape(n_dec, 1, SB_HEADS, SB_HEAD_DIM))
            vs_rows.append(dec_rows[:, 2 * d:].reshape(n_dec, 1, SB_HEADS, SB_HEAD_DIM))
        h = _proj_ln(mix, w_out, h, ln_mix_g[i], ln_mix_b[i], f"mix_out_ln_{i}")
        h = _ffn_ln(h, w_ffn_in[i], w_ffn_out[i], ln_ffn_g[i], ln_ffn_b[i], f"ffn_ln_{i}")

    y_prompt = h[:rows_prompt].reshape(batch, lp, d)[:, N_META:n_real]
    y_sample = h[rows_prompt:rows_prompt + n_dec].reshape(n_dec, 1, d)
    return (y_prompt, y_sample, jnp.stack(ret_p), jnp.stack(kp_rows), jnp.stack(vp_rows),
            jnp.stack(ret_s), jnp.stack(ks_rows), jnp.stack(vs_rows))
```

```python
import functools

import jax
import jax.numpy as jnp
from jax import lax
from jax.experimental import pallas as pl
from jax.experimental.pallas import tpu as pltpu

D_MODEL = 1024
N_META = 16
RET_HEADS = 4
RET_QK_DIM = D_MODEL // RET_HEADS
RET_V_DIM = 2 * RET_QK_DIM
ROPE_BASE = 10000.0
SB_HEADS = 16
SB_HEAD_DIM = D_MODEL // SB_HEADS
PAGE_SIZE = 128
DEPTH = 4
DEEPNORM_ALPHA = (2 * DEPTH) ** 0.25
LN_EPS = 1e-5

BLK = 128
ROW_TILE = 512
SB_CHUNK_BLOCKS = 5
SB_MASK_BLOCKS = 4
DEC_PAGES_PER_STEP = 8
VMEM_LIMIT = 56 * 1024 * 1024

F32 = jnp.float32
BF16 = jnp.bfloat16
NT = (((1,), (1,)), ((), ()))
TN = (((0,), (0,)), ((), ()))


def _params(*sem):
    return pltpu.CompilerParams(dimension_semantics=sem, vmem_limit_bytes=VMEM_LIMIT)


def _resident(shape):
    nd = len(shape)
    return pl.BlockSpec(shape, lambda *_: (0,) * nd, pipeline_mode=pl.Buffered(1))


def _layer_norm(y, g, b):
    mu = jnp.mean(y, axis=-1, keepdims=True)
    d = y - mu
    var = jnp.mean(d * d, axis=-1, keepdims=True)
    return d * lax.rsqrt(var + LN_EPS) * g + b


def _silu(x):
    return x * jax.nn.sigmoid(x)


def _proj_kernel(x_ref, w_ref, *refs, col_chunk, out_cols, n_unread):
    o_refs = refs[n_unread:]
    xb = x_ref[...].astype(BF16)
    for c in range(w_ref.shape[1] // col_chunk):
        lo = c * col_chunk
        acc = jnp.dot(xb, w_ref[:, lo:lo + col_chunk], preferred_element_type=F32)
        for o_ref, (first, width) in zip(o_refs, out_cols):
            if first <= lo < first + width:
                o_ref[:, lo - first:lo - first + col_chunk] = acc.astype(o_ref.dtype)


def _proj(x, w, outs, name, layer=0, n_layers=1, prev=()):
    rows, k = x.shape
    n = w.shape[1]
    col_chunk = 512
    assert all(first % col_chunk == 0 and width % col_chunk == 0 for _, first, width, _ in outs)
    out_specs, out_shape = [], []
    for dt, _, wd, per_layer in outs:
        if per_layer:
            out_specs.append(pl.BlockSpec((None, ROW_TILE, wd), lambda i: (layer, i, 0)))
            out_shape.append(jax.ShapeDtypeStruct((n_layers, rows, wd), dt))
        else:
            out_specs.append(pl.BlockSpec((ROW_TILE, wd), lambda i: (i, 0)))
            out_shape.append(jax.ShapeDtypeStruct((rows, wd), dt))
    slots = [o for o, out in enumerate(outs) if out[3]]
    assert len(prev) in (0, len(slots))
    kern = functools.partial(_proj_kernel, col_chunk=col_chunk, out_cols=tuple((f, wd) for _, f, wd, _ in outs),
                             n_unread=len(prev))
    return pl.pallas_call(
        kern,
        grid=(rows // ROW_TILE,),
        in_specs=[pl.BlockSpec((ROW_TILE, k), lambda i: (i, 0)), _resident((k, n))]
                 + [pl.BlockSpec(memory_space=pl.ANY)] * len(prev),
        out_specs=out_specs,
        out_shape=out_shape,
        input_output_aliases={2 + p: slots[p] for p in range(len(prev))},
        compiler_params=_params("parallel"),
        name=name,
    )(x, w, *prev)


def _mix_ffn_kernel(m_ref, wm_ref, h_ref, g1_ref, b1_ref, wi_ref, wo_ref, g2_ref, b2_ref, o_ref, acc_ref,
                    *, hid_chunk):
    mix = jnp.dot(m_ref[...], wm_ref[...], preferred_element_type=F32)
    x = _layer_norm(DEEPNORM_ALPHA * h_ref[...] + mix, g1_ref[...], b1_ref[...])
    xb = x.astype(BF16)
    hidden = wo_ref.shape[0]
    for c in range(hidden // hid_chunk):
        lo = c * hid_chunk
        a = jnp.dot(xb, wi_ref[:, lo:lo + hid_chunk], preferred_element_type=F32)
        b = jnp.dot(xb, wi_ref[:, hidden + lo:hidden + lo + hid_chunk], preferred_element_type=F32)
        part = jnp.dot((_silu(a) * b).astype(BF16), wo_ref[lo:lo + hid_chunk, :], preferred_element_type=F32)
        if c == 0:
            acc_ref[...] = part
        else:
            acc_ref[...] += part
    o_ref[...] = _layer_norm(DEEPNORM_ALPHA * x + acc_ref[...], g2_ref[...], b2_ref[...])


def _mix_ffn(mix, w_mix, h, g1, b1, w_in, w_out, g2, b2, name):
    rows, k = mix.shape
    d = w_mix.shape[1]
    hidden = w_out.shape[0]
    row_spec = pl.BlockSpec((ROW_TILE, d), lambda i: (i, 0))
    vec = lambda v: v.reshape(1, d)
    return pl.pallas_call(
        functools.partial(_mix_ffn_kernel, hid_chunk=256),
        grid=(rows // ROW_TILE,),
        in_specs=[pl.BlockSpec((ROW_TILE, k), lambda i: (i, 0)), _resident((k, d)), row_spec,
                  _resident((1, d)), _resident((1, d)),
                  _resident((d, 2 * hidden)), _resident((hidden, d)), _resident((1, d)), _resident((1, d))],
        out_specs=row_spec,
        out_shape=jax.ShapeDtypeStruct((rows, d), F32),
        scratch_shapes=[pltpu.VMEM((ROW_TILE, d), F32)],
        compiler_params=_params("parallel"),
        name=name,
    )(mix, w_mix, h, vec(g1), vec(b1), w_in, w_out, vec(g2), vec(b2))


def _rotate_lanes(x, cos, sin):
    half = x.shape[-1] // 2
    x1, x2 = x[:, :half], x[:, half:]
    return jnp.concatenate([x1 * cos - x2 * sin, x1 * sin + x2 * cos], axis=-1)


def _rms_gate(o, g):
    on = o * lax.rsqrt(jnp.mean(o * o, axis=-1, keepdims=True) + LN_EPS)
    return _silu(g) * on


def _state_slots(layer, n_layers, slot_shape, prev):
    zeros = (0,) * (len(slot_shape) - 1)
    spec = pl.BlockSpec((None, None) + slot_shape[1:], lambda i, *_: (layer, i) + zeros)
    shape = jax.ShapeDtypeStruct((n_layers,) + slot_shape, F32)
    extra_in = [] if prev is None else [pl.BlockSpec(memory_space=pl.ANY)]
    return spec, shape, extra_in


def _ret_prompt_kernel(x_ref, cos_ref, sin_ref, dec_ref, qd_ref, kd_ref, cd_ref, *rest, n_real):
    o_ref, st_ref, state_ref = rest[-3:]
    c = pl.program_id(1)

    @pl.when(c == 0)
    def _():
        state_ref[...] = jnp.zeros_like(state_ref)

    dk, dv = RET_QK_DIM, RET_V_DIM
    k0, v0, g0 = RET_HEADS * dk, 2 * RET_HEADS * dk, 2 * RET_HEADS * dk + RET_HEADS * dv
    cos, sin = cos_ref[...], sin_ref[...]
    row = c * BLK + lax.broadcasted_iota(jnp.int32, (BLK, 1), 0)
    for hh in range(RET_HEADS):
        q = _rotate_lanes(x_ref[:, hh * dk:(hh + 1) * dk].astype(F32), cos, sin)
        k = _rotate_lanes(x_ref[:, k0 + hh * dk:k0 + (hh + 1) * dk].astype(F32), cos, sin) * (dk ** -0.5)
        k = jnp.where(row < n_real, k, 0.0)
        vb = x_ref[:, v0 + hh * dv:v0 + (hh + 1) * dv]
        state = state_ref[hh]
        scores = lax.dot_general(q.astype(BF16), k.astype(BF16), NT, preferred_element_type=F32) * dec_ref[hh]
        o = (jnp.dot(scores.astype(BF16), vb, preferred_element_type=F32)
             + jnp.dot((q * qd_ref[hh]).astype(BF16), state.astype(BF16), preferred_element_type=F32))
        state_ref[hh] = cd_ref[hh][:, 0:1] * state + lax.dot_general(
            (k * kd_ref[hh]).astype(BF16), vb, TN, preferred_element_type=F32)
        g = x_ref[:, g0 + hh * dv:g0 + (hh + 1) * dv].astype(F32)
        o_ref[:, hh * dv:(hh + 1) * dv] = _rms_gate(o, g).astype(o_ref.dtype)

    @pl.when(c == pl.num_programs(1) - 1)
    def _():
        st_ref[...] = state_ref[...]


def _ret_prompt(qkvg, cos, sin, dec, qd, kd, cd, batch, nblk, rows, n_real, layer, n_layers, prev):
    h, dk, dv = RET_HEADS, RET_QK_DIM, RET_V_DIM
    last = lambda c: (c == nblk - 1).astype(jnp.int32)
    st_spec, st_shape, extra_in = _state_slots(layer, n_layers, (batch, h, dk, dv), prev)
    return pl.pallas_call(
        functools.partial(_ret_prompt_kernel, n_real=n_real),
        grid=(batch, nblk),
        in_specs=[
            pl.BlockSpec((BLK, qkvg.shape[1]), lambda b, c: (b * nblk + c, 0)),
            pl.BlockSpec((BLK, dk // 2), lambda b, c: (c, 0)),
            pl.BlockSpec((BLK, dk // 2), lambda b, c: (c, 0)),
            pl.BlockSpec((h, BLK, BLK), lambda b, c: (0, 0, 0)),
            pl.BlockSpec((h, BLK, 1), lambda b, c: (0, 0, 0)),
            pl.BlockSpec((None, h, BLK, 1), lambda b, c: (last(c), 0, 0, 0)),
            pl.BlockSpec((None, h, 1, BLK), lambda b, c: (last(c), 0, 0, 0)),
        ] + extra_in,
        out_specs=[pl.BlockSpec((BLK, h * dv), lambda b, c: (b * nblk + c, 0)), st_spec],
        out_shape=[jax.ShapeDtypeStruct((rows, h * dv), BF16), st_shape],
        scratch_shapes=[pltpu.VMEM((h, dk, dv), F32)],
        input_output_aliases={} if prev is None else {7: 1},
        compiler_params=_params("parallel", "arbitrary"),
        name="ret_prompt",
    )(qkvg, cos, sin, dec, qd, kd, cd, *([] if prev is None else [prev]))


def _ret_decode_kernel(q_ref, kr_ref, kc_ref, v_ref, g_ref, cr_ref, sr_ref, cc_ref, sc_ref, gam_ref,
                       st_ref, *rest):
    o_ref, so_ref = rest[-2:]
    dk, dv = RET_QK_DIM, RET_V_DIM
    scale = dk ** -0.5
    half = dk // 2
    cc, sc = cc_ref[...], sc_ref[...]
    for hh in range(RET_HEADS):
        q = _rotate_lanes(q_ref[:, hh * dk:(hh + 1) * dk], cr_ref[...], sr_ref[...])
        k_row = _rotate_lanes(kr_ref[:, hh * dk:(hh + 1) * dk], cr_ref[...], sr_ref[...]) * scale
        kc = kc_ref[hh]
        k1, k2 = kc[:half], kc[half:]
        k_col = jnp.concatenate([k1 * cc - k2 * sc, k1 * sc + k2 * cc], axis=0) * scale
        gam = gam_ref[hh]
        state = st_ref[hh]
        v = v_ref[:, hh * dv:(hh + 1) * dv]
        score = jnp.sum(q * k_row, axis=-1, keepdims=True)
        q_dec = jnp.broadcast_to(q * gam[:, :dk], (8, dk)).astype(BF16)
        o = score * v + jnp.dot(q_dec, state.astype(BF16), preferred_element_type=F32)[0:1]
        so_ref[hh] = gam * state + k_col * v
        o_ref[:, hh * dv:(hh + 1) * dv] = _rms_gate(o, g_ref[:, hh * dv:(hh + 1) * dv])


def _ret_decode(dec_rows, state, cos_row, sin_row, gam, layer, n_layers, prev):
    n = dec_rows.shape[0]
    h, dk, dv = RET_HEADS, RET_QK_DIM, RET_V_DIM
    q = dec_rows[:, :h * dk].reshape(n, 1, h * dk)
    k = dec_rows[:, h * dk:2 * h * dk]
    v = dec_rows[:, 2 * h * dk:2 * h * dk + h * dv].reshape(n, 1, h * dv)
    g = dec_rows[:, 2 * h * dk + h * dv:].reshape(n, 1, h * dv)
    row_qk = pl.BlockSpec((None, 1, h * dk), lambda s: (s, 0, 0))
    row_v = pl.BlockSpec((None, 1, h * dv), lambda s: (s, 0, 0))
    const2 = lambda shape: pl.BlockSpec(shape, lambda s: (0, 0))
    so_spec, so_shape, extra_in = _state_slots(layer, n_layers, (n, h, dk, dv), prev)
    return pl.pallas_call(
        _ret_decode_kernel,
        grid=(n,),
        in_specs=[row_qk, row_qk, pl.BlockSpec((None, h, dk, 1), lambda s: (s, 0, 0, 0)), row_v, row_v,
                  const2((1, dk // 2)), const2((1, dk // 2)), const2((dk // 2, 1)), const2((dk // 2, 1)),
                  pl.BlockSpec((h, 1, dv), lambda s: (0, 0, 0)),
                  pl.BlockSpec((None, None, h, dk, dv), lambda s: (layer, s, 0, 0, 0))] + extra_in,
        out_specs=[row_v, so_spec],
        out_shape=[jax.ShapeDtypeStruct((n, 1, h * dv), F32), so_shape],
        input_output_aliases={} if prev is None else {11: 1},
        compiler_params=_params("parallel"),
        name="ret_decode",
    )(q, k.reshape(n, 1, h * dk), k.reshape(n, h, dk, 1), v, g,
      cos_row, sin_row, cos_row.reshape(-1, 1), sin_row.reshape(-1, 1), gam, state,
      *([] if prev is None else [prev]))


def _softplus(z):
    return jnp.maximum(z, 0.0) + jnp.log(1.0 + jnp.exp(-jnp.abs(z)))


def _suffix_sums(keep, tri2):
    hi = keep.astype(BF16)
    lo = (keep - hi.astype(F32)).astype(BF16)
    return jnp.dot(jnp.concatenate([hi, lo], axis=1), tri2, preferred_element_type=F32)


def _sb_prompt_kernel(q_ref, k_ref, v_ref, bias_ref, tri2_ref, o_ref, qh_ref, acc_ref, carry_ref, *, cases):
    step = pl.program_id(2)
    nblk = pl.num_programs(2)
    j = nblk - 1 - step

    @pl.when(step == 0)
    def _():
        lane = lax.broadcasted_iota(jnp.int32, q_ref.shape, 1)
        qf = q_ref[...].astype(F32) * (SB_HEAD_DIM ** -0.5)
        qh_ref[0] = jnp.where(lane < SB_HEAD_DIM, qf, 0.0).astype(BF16)
        qh_ref[1] = jnp.where(lane >= SB_HEAD_DIM, qf, 0.0).astype(BF16)
        acc_ref[...] = jnp.zeros_like(acc_ref)
        carry_ref[...] = jnp.zeros_like(carry_ref)

    def scores(first_blk, n_blk, masked):
        r0, nr = first_blk * BLK, n_blk * BLK
        rows = slice(r0, r0 + nr)
        q2 = jnp.concatenate([qh_ref[0, rows, :], qh_ref[1, rows, :]], axis=0)
        z = lax.dot_general(q2, k_ref[...], NT, preferred_element_type=F32)
        return dict(r0=r0, nr=nr, rows=rows, masked=masked, z=z)

    def suffix(c):
        nr = c["nr"]
        bias2 = jnp.concatenate([jnp.broadcast_to(bias_ref[0], (nr, BLK)),
                                 jnp.broadcast_to(bias_ref[1], (nr, BLK))], axis=0)
        c["z"] = z = c["z"] + bias2
        keep = _softplus(z)
        if c["masked"]:
            lane = lax.broadcasted_iota(jnp.int32, (nr, BLK), 1)
            earlier = (lane - lax.broadcasted_iota(jnp.int32, (nr, BLK), 0)) < (c["r0"] - j * BLK)
            c["visible"] = jnp.concatenate([earlier, earlier], axis=0)
            keep = jnp.where(c["visible"], keep, 0.0)
        c["sums"] = _suffix_sums(keep, tri2_ref[...])

    def weights(c):
        nr, rows = c["nr"], c["rows"]
        carry = jnp.concatenate([carry_ref[0, rows, :], carry_ref[1, rows, :]], axis=0)
        a = jnp.exp(c["z"] - c["sums"][:, :BLK] - carry)
        if c["masked"]:
            a = jnp.where(c["visible"], a, 0.0)
        pv = jnp.dot(a.astype(BF16), v_ref[...], preferred_element_type=F32)
        lane = lax.broadcasted_iota(jnp.int32, (nr, BLK), 1)
        c["acc"] = acc_ref[rows, :] + jnp.where(lane < SB_HEAD_DIM, pv[:nr], pv[nr:])
        c["carry"] = carry + c["sums"][:, BLK:]

    for (first_blk, n_blk), group, above in cases:
        @pl.when((j >= first_blk) & (j < first_blk + n_blk))
        def _(group=group, above=above):
            todo = [(f, n, True) for f, n in group] + [(f, n, False) for f, n in above]
            chunks = []
            for t in range(len(todo) + 2):
                if t < len(todo):
                    chunks.append(scores(*todo[t]))
                if 0 <= t - 1 < len(todo):
                    suffix(chunks[t - 1])
                if 0 <= t - 2 < len(todo):
                    weights(chunks[t - 2])
            for c in chunks:
                nr, rows = c["nr"], c["rows"]
                acc_ref[rows, :] = c["acc"]
                carry_ref[0, rows, :] = c["carry"][:nr]
                carry_ref[1, rows, :] = c["carry"][nr:]

    @pl.when(step == nblk - 1)
    def _():
        o_ref[...] = acc_ref[...].astype(o_ref.dtype)


def _split_blocks(first, count):
    pieces = pl.cdiv(count, SB_CHUNK_BLOCKS)
    out = []
    for p in range(pieces):
        size = count // pieces + (1 if p < count % pieces else 0)
        out.append((first, size))
        first += size
    return tuple(out)


def _sb_cases(nblk):
    cases = []
    for first in range(0, max(nblk - nblk % SB_MASK_BLOCKS, 1), SB_MASK_BLOCKS):
        n = SB_MASK_BLOCKS if first + 2 * SB_MASK_BLOCKS <= nblk else nblk - first
        cases.append(((first, n), _split_blocks(first, n), _split_blocks(first + n, nblk - first - n)))
    return tuple(cases)


def _sb_prompt(qkv, bias_rows, tri2, batch, nblk, rows):
    lp = nblk * BLK
    pairs = D_MODEL // BLK
    return pl.pallas_call(
        functools.partial(_sb_prompt_kernel, cases=_sb_cases(nblk)),
        grid=(batch, pairs, nblk),
        in_specs=[
            pl.BlockSpec((lp, BLK), lambda b, p, s: (b, p)),
            pl.BlockSpec((BLK, BLK), lambda b, p, s: (b * nblk + nblk - 1 - s, pairs + p)),
            pl.BlockSpec((BLK, BLK), lambda b, p, s: (b * nblk + nblk - 1 - s, 2 * pairs + p)),
            pl.BlockSpec((2, 1, BLK), lambda b, p, s: (p, 0, 0)),
            pl.BlockSpec((2 * BLK, 2 * BLK), lambda b, p, s: (0, 0)),
        ],
        out_specs=pl.BlockSpec((lp, BLK), lambda b, p, s: (b, p)),
        out_shape=jax.ShapeDtypeStruct((rows, D_MODEL), BF16),
        scratch_shapes=[pltpu.VMEM((2, lp, BLK), BF16), pltpu.VMEM((lp, BLK), F32),
                        pltpu.VMEM((2, lp, BLK), F32)],
        compiler_params=_params("parallel", "parallel", "arbitrary"),
        name="sb_prompt",
    )(qkv, qkv, qkv, bias_rows, tri2)


def _sb_decode_kernel(pt_ref, q_ref, *refs, n_pg):
    del pt_ref
    k_refs, v_refs = refs[:n_pg], refs[n_pg:2 * n_pg]
    bias_ref, tri2_ref, o_ref, acc_ref, carry_ref = refs[2 * n_pg:]
    g = pl.program_id(1)

    @pl.when(g == 0)
    def _():
        acc_ref[...] = jnp.zeros_like(acc_ref)
        carry_ref[...] = jnp.zeros_like(carry_ref)

    lane_d = lax.broadcasted_iota(jnp.int32, (SB_HEADS, D_MODEL), 1)
    head_d = lax.broadcasted_iota(jnp.int32, (SB_HEADS, D_MODEL), 0)
    own = lax.shift_right_logical(lane_d, 6) == head_d
    q = q_ref[...] * (SB_HEAD_DIM ** -0.5)
    q_heads = jnp.where(own, jnp.broadcast_to(q, (SB_HEADS, D_MODEL)), 0.0).astype(BF16)

    z = jnp.concatenate(
        [jnp.dot(q_heads, k_refs[p][...].reshape(D_MODEL, PAGE_SIZE).astype(BF16), preferred_element_type=F32)
         for p in range(n_pg)], axis=0) + jnp.concatenate([bias_ref[...]] * n_pg, axis=0)
    sums = _suffix_sums(_softplus(z), tri2_ref[...])
    carry = carry_ref[...]
    acc = acc_ref[...]
    for p in range(n_pg):
        rows = slice(p * SB_HEADS, (p + 1) * SB_HEADS)
        a = jnp.exp(z[rows] - sums[rows, :PAGE_SIZE] - carry)
        acc = acc + lax.dot_general(a.astype(BF16), v_refs[p][...].reshape(D_MODEL, PAGE_SIZE).astype(BF16),
                                    NT, preferred_element_type=F32)
        carry = carry + sums[rows, PAGE_SIZE:]
    acc_ref[...] = acc
    carry_ref[...] = carry

    @pl.when(g == pl.num_programs(1) - 1)
    def _():
        o_ref[...] = jnp.sum(jnp.where(own, acc, 0.0), axis=0, keepdims=True)


def _sb_decode(q, cache_k, cache_v, layer, page_table, bias_col, tri2):
    n, n_pages = page_table.shape
    n_pg = DEC_PAGES_PER_STEP
    assert n_pages % n_pg == 0
    ck = jnp.transpose(cache_k, (0, 1, 3, 4, 2))
    cv = jnp.transpose(cache_v, (0, 1, 3, 4, 2))

    def page_spec(p):
        return pl.BlockSpec((None, None, SB_HEADS, SB_HEAD_DIM, PAGE_SIZE),
                            lambda s, g, pt: (layer, pt[s, n_pages - 1 - (g * n_pg + p)], 0, 0, 0))

    row_spec = pl.BlockSpec((None, 1, D_MODEL), lambda s, g, pt: (s, 0, 0))
    return pl.pallas_call(
        functools.partial(_sb_decode_kernel, n_pg=n_pg),
        grid_spec=pltpu.PrefetchScalarGridSpec(
            num_scalar_prefetch=1,
            grid=(n, n_pages // n_pg),
            in_specs=[row_spec] + [page_spec(p) for p in range(n_pg)] * 2
                     + [pl.BlockSpec((SB_HEADS, 1), lambda s, g, pt: (0, 0)),
                        pl.BlockSpec((2 * BLK, 2 * BLK), lambda s, g, pt: (0, 0))],
            out_specs=row_spec,
            scratch_shapes=[pltpu.VMEM((SB_HEADS, D_MODEL), F32), pltpu.VMEM((SB_HEADS, PAGE_SIZE), F32)],
        ),
        out_shape=jax.ShapeDtypeStruct((n, 1, D_MODEL), F32),
        compiler_params=_params("parallel", "arbitrary"),
        name="sb_decode",
    )(page_table, q, *([ck] * n_pg), *([cv] * n_pg), bias_col, tri2)


def _retention_tables(n_last):
    log_g = jnp.log(1.0 - 2.0 ** (-5.0 - jnp.arange(RET_HEADS, dtype=F32)))
    idx = jnp.arange(BLK, dtype=F32)
    diff = idx[:, None] - idx[None, :]
    lg = log_g[:, None, None]
    dec = jnp.where(diff >= 0, jnp.exp(lg * jnp.maximum(diff, 0.0)), 0.0)
    qd = jnp.exp(log_g[:, None] * (idx[None, :] + 1.0))[:, :, None]
    n_tok = jnp.array([BLK, n_last], F32)[:, None, None]
    kd = jnp.exp(log_g[None, :, None] * (n_tok - 1.0 - idx[None, None, :]))[..., None]
    cd = jnp.broadcast_to(jnp.exp(log_g[None, :, None] * n_tok)[..., None], (2, RET_HEADS, 1, BLK))
    gam = jnp.broadcast_to(jnp.exp(log_g)[:, None, None], (RET_HEADS, 1, RET_V_DIM))
    return dec, qd, kd, cd, gam


def _rope_tables(pos):
    half = RET_QK_DIM // 2
    inv = ROPE_BASE ** (-jnp.arange(half, dtype=F32) / half)
    ang = pos.astype(F32)[:, None] * inv[None, :]
    return jnp.cos(ang), jnp.sin(ang)


def _with_decode_rows(prompt_rows_out, dec_rows, rows_prompt):
    n, width = dec_rows.shape
    tile = jnp.concatenate([dec_rows, jnp.zeros((ROW_TILE - n, width), dec_rows.dtype)], axis=0)
    return lax.dynamic_update_slice(prompt_rows_out, tile.astype(prompt_rows_out.dtype), (rows_prompt, 0))


def kernel(x_prompt, x_sample, state_ret, cache_k, cache_v, page_table, meta_tokens, w_ret_in, w_ret_out,
           w_sb_qkv, w_sb_out, sb_bias, w_ffn_in, w_ffn_out, ln_mix_g, ln_mix_b, ln_ffn_g, ln_ffn_b):
    batch, seq, d = x_prompt.shape
    n_dec = x_sample.shape[0]
    assert d == D_MODEL and x_sample.shape[1] == 1 and n_dec <= ROW_TILE
    assert cache_k.shape[2] == PAGE_SIZE
    n_real = N_META + seq
    nblk = pl.cdiv(n_real, BLK)
    lp = nblk * BLK
    rows_prompt = batch * lp
    assert rows_prompt % ROW_TILE == 0
    rows = rows_prompt + ROW_TILE
    past_len = page_table.shape[1] * PAGE_SIZE

    w_ret_in, w_ret_out, w_sb_qkv, w_sb_out, w_ffn_in, w_ffn_out = (
        w.astype(BF16) for w in (w_ret_in, w_ret_out, w_sb_qkv, w_sb_out, w_ffn_in, w_ffn_out))

    hp = jnp.concatenate([jnp.broadcast_to(meta_tokens[None].astype(F32), (batch, N_META, d)), x_prompt,
                          jnp.zeros((batch, lp - n_real, d), F32)], axis=1).reshape(rows_prompt, d)
    h = jnp.concatenate([hp, x_sample.reshape(n_dec, d), jnp.zeros((ROW_TILE - n_dec, d), F32)], axis=0)

    dec, qd, kd, cd, gam = _retention_tables(n_real - (nblk - 1) * BLK)
    cos_p, sin_p = _rope_tables(jnp.arange(lp))
    cos_s, sin_s = _rope_tables(past_len + jnp.arange(1))
    idx = jnp.arange(BLK)
    tri = jnp.concatenate([idx[:, None] >= idx[None, :], jnp.ones((BLK, BLK), bool)], axis=1)
    tri2 = jnp.concatenate([tri, tri], axis=0).astype(BF16)

    n_ret, n_sb = (DEPTH + 1) // 2, DEPTH // 2
    ret_p = ret_s = None
    kv_rows = ()
    for i in range(DEPTH):
        j = i // 2
        if i % 2 == 0:
            (qkvg,) = _proj(h, w_ret_in[j], [(BF16, 0, w_ret_in.shape[2], False)], f"ret_in_{j}")
            mix_p, ret_p = _ret_prompt(qkvg, cos_p, sin_p, dec, qd, kd, cd, batch, nblk, rows, n_real,
                                       j, n_ret, ret_p)
            mix_s, ret_s = _ret_decode(qkvg[rows_prompt:rows_prompt + n_dec].astype(F32), state_ret,
                                       cos_s, sin_s, gam, j, n_ret, ret_s)
            mix = _with_decode_rows(mix_p, mix_s.reshape(n_dec, -1), rows_prompt)
            w_mix = w_ret_out[j]
        else:
            qkv_b, *kv_rows = _proj(h, w_sb_qkv[j],
                                    [(BF16, 0, 3 * d, False), (F32, d, d, True), (F32, 2 * d, d, True)],
                                    f"sb_qkv_{j}", j, n_sb, tuple(kv_rows))
            bias = sb_bias[j].astype(F32)
            mix_p = _sb_prompt(qkv_b, jnp.broadcast_to(bias[:, None, None], (SB_HEADS, 1, BLK)), tri2,
                               batch, nblk, rows)
            q_s = qkv_b[rows_prompt:rows_prompt + n_dec, :d].astype(F32)
            mix_s = _sb_decode(q_s.reshape(n_dec, 1, d), cache_k, cache_v, j, page_table,
                               bias.reshape(SB_HEADS, 1), tri2)
            mix = _with_decode_rows(mix_p, mix_s.reshape(n_dec, d), rows_prompt)
            w_mix = w_sb_out[j]
        h = _mix_ffn(mix, w_mix, h, ln_mix_g[i], ln_mix_b[i], w_ffn_in[i], w_ffn_out[i],
                     ln_ffn_g[i], ln_ffn_b[i], f"mix_ffn_{i}")

    y_prompt = h[:rows_prompt].reshape(batch, lp, d)[:, N_META:n_real]
    y_sample = h[rows_prompt:rows_prompt + n_dec].reshape(n_dec, 1, d)
    prompt_rows = lambda t: t[:, :rows_prompt].reshape(n_sb, batch, lp, SB_HEADS, SB_HEAD_DIM)[:, :, :n_real]
    decode_rows = lambda t: t[:, rows_prompt:rows_prompt + n_dec].reshape(n_sb, n_dec, 1, SB_HEADS, SB_HEAD_DIM)
    k_rows, v_rows = kv_rows
    return (y_prompt, y_sample, ret_p, prompt_rows(k_rows), prompt_rows(v_rows),
            ret_s, decode_rows(k_rows), decode_rows(v_rows))
```

```python
import functools

import jax
import jax.numpy as jnp
from jax import lax
from jax.experimental import pallas as pl
from jax.experimental.pallas import tpu as pltpu

D_MODEL = 1024
N_META = 16
RET_HEADS = 4
RET_QK_DIM = D_MODEL // RET_HEADS
RET_V_DIM = 2 * RET_QK_DIM
ROPE_BASE = 10000.0
SB_HEADS = 16
SB_HEAD_DIM = D_MODEL // SB_HEADS
PAGE_SIZE = 128
DEPTH = 4
DEEPNORM_ALPHA = (2 * DEPTH) ** 0.25
LN_EPS = 1e-5

BLK = 128
ROW_TILE = 512
SB_CHUNK_BLOCKS = 5
SB_MASK_BLOCKS = 4
SB_PAIRS_PER_STEP = 2
DEC_PAGES_PER_STEP = 8
VMEM_LIMIT = 56 * 1024 * 1024

F32 = jnp.float32
BF16 = jnp.bfloat16
NT = (((1,), (1,)), ((), ()))
TN = (((0,), (0,)), ((), ()))


def _params(*sem):
    return pltpu.CompilerParams(dimension_semantics=sem, vmem_limit_bytes=VMEM_LIMIT)


def _resident(shape):
    nd = len(shape)
    return pl.BlockSpec(shape, lambda *_: (0,) * nd, pipeline_mode=pl.Buffered(1))


def _layer_norm(y, g, b):
    mu = jnp.mean(y, axis=-1, keepdims=True)
    d = y - mu
    var = jnp.mean(d * d, axis=-1, keepdims=True)
    return d * lax.rsqrt(var + LN_EPS) * g + b


def _silu(x):
    return x * jax.nn.sigmoid(x)


def _proj_kernel(x_ref, w_ref, *refs, col_chunk, out_cols, n_unread):
    o_refs = refs[n_unread:]
    xb = x_ref[...].astype(BF16)
    for c in range(w_ref.shape[1] // col_chunk):
        lo = c * col_chunk
        acc = jnp.dot(xb, w_ref[:, lo:lo + col_chunk], preferred_element_type=F32)
        for o_ref, (first, width) in zip(o_refs, out_cols):
            if first <= lo < first + width:
                o_ref[:, lo - first:lo - first + col_chunk] = acc.astype(o_ref.dtype)


def _proj(x, w, outs, name, layer=0, n_layers=1, prev=()):
    rows, k = x.shape
    n = w.shape[1]
    col_chunk = 512
    assert all(first % col_chunk == 0 and width % col_chunk == 0 for _, first, width, _ in outs)
    out_specs, out_shape = [], []
    for dt, _, wd, per_layer in outs:
        if per_layer:
            out_specs.append(pl.BlockSpec((None, ROW_TILE, wd), lambda i: (layer, i, 0)))
            out_shape.append(jax.ShapeDtypeStruct((n_layers, rows, wd), dt))
        else:
            out_specs.append(pl.BlockSpec((ROW_TILE, wd), lambda i: (i, 0)))
            out_shape.append(jax.ShapeDtypeStruct((rows, wd), dt))
    slots = [o for o, out in enumerate(outs) if out[3]]
    assert len(prev) in (0, len(slots))
    kern = functools.partial(_proj_kernel, col_chunk=col_chunk, out_cols=tuple((f, wd) for _, f, wd, _ in outs),
                             n_unread=len(prev))
    return pl.pallas_call(
        kern,
        grid=(rows // ROW_TILE,),
        in_specs=[pl.BlockSpec((ROW_TILE, k), lambda i: (i, 0)), _resident((k, n))]
                 + [pl.BlockSpec(memory_space=pl.ANY)] * len(prev),
        out_specs=out_specs,
        out_shape=out_shape,
        input_output_aliases={2 + p: slots[p] for p in range(len(prev))},
        compiler_params=_params("parallel"),
        name=name,
    )(x, w, *prev)


def _mix_ffn_kernel(m_ref, wm_ref, h_ref, g1_ref, b1_ref, wi_ref, wo_ref, g2_ref, b2_ref, o_ref, acc_ref,
                    *, hid_chunk):
    mix = jnp.dot(m_ref[...], wm_ref[...], preferred_element_type=F32)
    x = _layer_norm(DEEPNORM_ALPHA * h_ref[...] + mix, g1_ref[...], b1_ref[...])
    xb = x.astype(BF16)
    hidden = wo_ref.shape[0]
    for c in range(hidden // hid_chunk):
        lo = c * hid_chunk
        a = jnp.dot(xb, wi_ref[:, lo:lo + hid_chunk], preferred_element_type=F32)
        b = jnp.dot(xb, wi_ref[:, hidden + lo:hidden + lo + hid_chunk], preferred_element_type=F32)
        part = jnp.dot((_silu(a) * b).astype(BF16), wo_ref[lo:lo + hid_chunk, :], preferred_element_type=F32)
        if c == 0:
            acc_ref[...] = part
        else:
            acc_ref[...] += part
    o_ref[...] = _layer_norm(DEEPNORM_ALPHA * x + acc_ref[...], g2_ref[...], b2_ref[...])


def _mix_ffn(mix, w_mix, h, g1, b1, w_in, w_out, g2, b2, name):
    rows, k = mix.shape
    d = w_mix.shape[1]
    hidden = w_out.shape[0]
    row_spec = pl.BlockSpec((ROW_TILE, d), lambda i: (i, 0))
    vec = lambda v: v.reshape(1, d)
    return pl.pallas_call(
        functools.partial(_mix_ffn_kernel, hid_chunk=256),
        grid=(rows // ROW_TILE,),
        in_specs=[pl.BlockSpec((ROW_TILE, k), lambda i: (i, 0)), _resident((k, d)), row_spec,
                  _resident((1, d)), _resident((1, d)),
                  _resident((d, 2 * hidden)), _resident((hidden, d)), _resident((1, d)), _resident((1, d))],
        out_specs=row_spec,
        out_shape=jax.ShapeDtypeStruct((rows, d), F32),
        scratch_shapes=[pltpu.VMEM((ROW_TILE, d), F32)],
        compiler_params=_params("parallel"),
        name=name,
    )(mix, w_mix, h, vec(g1), vec(b1), w_in, w_out, vec(g2), vec(b2))


def _rotate_lanes(x, cos, sin):
    half = x.shape[-1] // 2
    x1, x2 = x[:, :half], x[:, half:]
    return jnp.concatenate([x1 * cos - x2 * sin, x1 * sin + x2 * cos], axis=-1)


def _rms_gate(o, g):
    on = o * lax.rsqrt(jnp.mean(o * o, axis=-1, keepdims=True) + LN_EPS)
    return _silu(g) * on


def _state_slots(layer, n_layers, slot_shape, prev):
    zeros = (0,) * (len(slot_shape) - 1)
    spec = pl.BlockSpec((None, None) + slot_shape[1:], lambda i, *_: (layer, i) + zeros)
    shape = jax.ShapeDtypeStruct((n_layers,) + slot_shape, F32)
    extra_in = [] if prev is None else [pl.BlockSpec(memory_space=pl.ANY)]
    return spec, shape, extra_in


def _ret_prompt_kernel(x_ref, cos_ref, sin_ref, dec_ref, qd_ref, kd_ref, cd_ref, *rest, n_real):
    o_ref, st_ref, state_ref = rest[-3:]
    c = pl.program_id(1)

    @pl.when(c == 0)
    def _():
        state_ref[...] = jnp.zeros_like(state_ref)

    dk, dv = RET_QK_DIM, RET_V_DIM
    k0, v0, g0 = RET_HEADS * dk, 2 * RET_HEADS * dk, 2 * RET_HEADS * dk + RET_HEADS * dv
    cos, sin = cos_ref[...], sin_ref[...]
    row = c * BLK + lax.broadcasted_iota(jnp.int32, (BLK, 1), 0)

    def scores(hh):
        q = _rotate_lanes(x_ref[:, hh * dk:(hh + 1) * dk].astype(F32), cos, sin)
        k = _rotate_lanes(x_ref[:, k0 + hh * dk:k0 + (hh + 1) * dk].astype(F32), cos, sin) * (dk ** -0.5)
        k = jnp.where(row < n_real, k, 0.0)
        s = lax.dot_general(q.astype(BF16), k.astype(BF16), NT, preferred_element_type=F32) * dec_ref[hh]
        return dict(hh=hh, q=q, k=k, s=s, vb=x_ref[:, v0 + hh * dv:v0 + (hh + 1) * dv], state=state_ref[hh])

    def output(c_):
        hh = c_["hh"]
        o = (jnp.dot(c_["s"].astype(BF16), c_["vb"], preferred_element_type=F32)
             + jnp.dot((c_["q"] * qd_ref[hh]).astype(BF16), c_["state"].astype(BF16),
                       preferred_element_type=F32))
        g = x_ref[:, g0 + hh * dv:g0 + (hh + 1) * dv].astype(F32)
        o_ref[:, hh * dv:(hh + 1) * dv] = _rms_gate(o, g).astype(o_ref.dtype)

    def update(c_):
        hh = c_["hh"]
        state_ref[hh] = cd_ref[hh][:, 0:1] * c_["state"] + lax.dot_general(
            (c_["k"] * kd_ref[hh]).astype(BF16), c_["vb"], TN, preferred_element_type=F32)

    heads = []
    for t in range(RET_HEADS + 2):
        if t < RET_HEADS:
            heads.append(scores(t))
        if 0 <= t - 1 < RET_HEADS:
            output(heads[t - 1])
        if 0 <= t - 2 < RET_HEADS:
            update(heads[t - 2])

    @pl.when(c == pl.num_programs(1) - 1)
    def _():
        st_ref[...] = state_ref[...]


def _ret_prompt(qkvg, cos, sin, dec, qd, kd, cd, batch, nblk, rows, n_real, layer, n_layers, prev):
    h, dk, dv = RET_HEADS, RET_QK_DIM, RET_V_DIM
    last = lambda c: (c == nblk - 1).astype(jnp.int32)
    st_spec, st_shape, extra_in = _state_slots(layer, n_layers, (batch, h, dk, dv), prev)
    return pl.pallas_call(
        functools.partial(_ret_prompt_kernel, n_real=n_real),
        grid=(batch, nblk),
        in_specs=[
            pl.BlockSpec((BLK, qkvg.shape[1]), lambda b, c: (b * nblk + c, 0)),
            pl.BlockSpec((BLK, dk // 2), lambda b, c: (c, 0)),
            pl.BlockSpec((BLK, dk // 2), lambda b, c: (c, 0)),
            pl.BlockSpec((h, BLK, BLK), lambda b, c: (0, 0, 0)),
            pl.BlockSpec((h, BLK, 1), lambda b, c: (0, 0, 0)),
            pl.BlockSpec((None, h, BLK, 1), lambda b, c: (last(c), 0, 0, 0)),
            pl.BlockSpec((None, h, 1, BLK), lambda b, c: (last(c), 0, 0, 0)),
        ] + extra_in,
        out_specs=[pl.BlockSpec((BLK, h * dv), lambda b, c: (b * nblk + c, 0)), st_spec],
        out_shape=[jax.ShapeDtypeStruct((rows, h * dv), BF16), st_shape],
        scratch_shapes=[pltpu.VMEM((h, dk, dv), F32)],
        input_output_aliases={} if prev is None else {7: 1},
        compiler_params=_params("parallel", "arbitrary"),
        name="ret_prompt",
    )(qkvg, cos, sin, dec, qd, kd, cd, *([] if prev is None else [prev]))


def _ret_decode_kernel(q_ref, kr_ref, kc_ref, v_ref, g_ref, cr_ref, sr_ref, cc_ref, sc_ref, gam_ref,
                       st_ref, *rest):
    o_ref, so_ref = rest[-2:]
    dk, dv = RET_QK_DIM, RET_V_DIM
    scale = dk ** -0.5
    half = dk // 2
    cc, sc = cc_ref[...], sc_ref[...]
    for hh in range(RET_HEADS):
        q = _rotate_lanes(q_ref[:, hh * dk:(hh + 1) * dk], cr_ref[...], sr_ref[...])
        k_row = _rotate_lanes(kr_ref[:, hh * dk:(hh + 1) * dk], cr_ref[...], sr_ref[...]) * scale
        kc = kc_ref[hh]
        k1, k2 = kc[:half], kc[half:]
        k_col = jnp.concatenate([k1 * cc - k2 * sc, k1 * sc + k2 * cc], axis=0) * scale
        gam = gam_ref[hh]
        state = st_ref[hh]
        v = v_ref[:, hh * dv:(hh + 1) * dv]
        score = jnp.sum(q * k_row, axis=-1, keepdims=True)
        q_dec = jnp.broadcast_to(q * gam[:, :dk], (8, dk)).astype(BF16)
        o = score * v + jnp.dot(q_dec, state.astype(BF16), preferred_element_type=F32)[0:1]
        so_ref[hh] = gam * state + k_col * v
        o_ref[:, hh * dv:(hh + 1) * dv] = _rms_gate(o, g_ref[:, hh * dv:(hh + 1) * dv])


def _ret_decode(dec_rows, state, cos_row, sin_row, gam, layer, n_layers, prev):
    n = dec_rows.shape[0]
    h, dk, dv = RET_HEADS, RET_QK_DIM, RET_V_DIM
    q = dec_rows[:, :h * dk].reshape(n, 1, h * dk)
    k = dec_rows[:, h * dk:2 * h * dk]
    v = dec_rows[:, 2 * h * dk:2 * h * dk + h * dv].reshape(n, 1, h * dv)
    g = dec_rows[:, 2 * h * dk + h * dv:].reshape(n, 1, h * dv)
    row_qk = pl.BlockSpec((None, 1, h * dk), lambda s: (s, 0, 0))
    row_v = pl.BlockSpec((None, 1, h * dv), lambda s: (s, 0, 0))
    const2 = lambda shape: pl.BlockSpec(shape, lambda s: (0, 0))
    so_spec, so_shape, extra_in = _state_slots(layer, n_layers, (n, h, dk, dv), prev)
    return pl.pallas_call(
        _ret_decode_kernel,
        grid=(n,),
        in_specs=[row_qk, row_qk, pl.BlockSpec((None, h, dk, 1), lambda s: (s, 0, 0, 0)), row_v, row_v,
                  const2((1, dk // 2)), const2((1, dk // 2)), const2((dk // 2, 1)), const2((dk // 2, 1)),
                  pl.BlockSpec((h, 1, dv), lambda s: (0, 0, 0)),
                  pl.BlockSpec((None, None, h, dk, dv), lambda s: (layer, s, 0, 0, 0))] + extra_in,
        out_specs=[row_v, so_spec],
        out_shape=[jax.ShapeDtypeStruct((n, 1, h * dv), F32), so_shape],
        input_output_aliases={} if prev is None else {11: 1},
        compiler_params=_params("parallel"),
        name="ret_decode",
    )(q, k.reshape(n, 1, h * dk), k.reshape(n, h, dk, 1), v, g,
      cos_row, sin_row, cos_row.reshape(-1, 1), sin_row.reshape(-1, 1), gam, state,
      *([] if prev is None else [prev]))


def _softplus(z):
    return jnp.maximum(z, 0.0) + jnp.log(1.0 + jnp.exp(-jnp.abs(z)))


def _suffix_sums(keep, tri2):
    hi = keep.astype(BF16)
    lo = (keep - hi.astype(F32)).astype(BF16)
    return jnp.dot(jnp.concatenate([hi, lo], axis=1), tri2, preferred_element_type=F32)


def _sb_prompt_kernel(q_ref, k_ref, v_ref, bias_ref, tri2_ref, o_ref, qh_ref, acc_ref, carry_ref, *, cases):
    step = pl.program_id(2)
    nblk = pl.num_programs(2)
    j = nblk - 1 - step

    @pl.when(step == 0)
    def _():
        lane = lax.broadcasted_iota(jnp.int32, (q_ref.shape[0], BLK), 1)
        for p in range(SB_PAIRS_PER_STEP):
            qf = q_ref[:, p * BLK:(p + 1) * BLK].astype(F32) * (SB_HEAD_DIM ** -0.5)
            qh_ref[2 * p] = jnp.where(lane < SB_HEAD_DIM, qf, 0.0).astype(BF16)
            qh_ref[2 * p + 1] = jnp.where(lane >= SB_HEAD_DIM, qf, 0.0).astype(BF16)
        acc_ref[...] = jnp.zeros_like(acc_ref)
        carry_ref[...] = jnp.zeros_like(carry_ref)

    def scores(first_blk, n_blk, masked, p):
        r0, nr = first_blk * BLK, n_blk * BLK
        rows = slice(r0, r0 + nr)
        q2 = jnp.concatenate([qh_ref[2 * p, rows, :], qh_ref[2 * p + 1, rows, :]], axis=0)
        z = lax.dot_general(q2, k_ref[:, p * BLK:(p + 1) * BLK], NT, preferred_element_type=F32)
        return dict(r0=r0, nr=nr, rows=rows, masked=masked, p=p, z=z)

    def suffix(c):
        nr, p = c["nr"], c["p"]
        z = c["z"] + jnp.concatenate([jnp.broadcast_to(bias_ref[2 * p], (nr, BLK)),
                                      jnp.broadcast_to(bias_ref[2 * p + 1], (nr, BLK))], axis=0)
        if c["masked"]:
            lane = lax.broadcasted_iota(jnp.int32, (nr, BLK), 1)
            earlier = (lane - lax.broadcasted_iota(jnp.int32, (nr, BLK), 0)) < (c["r0"] - j * BLK)
            z = jnp.where(jnp.concatenate([earlier, earlier], axis=0), z, -1e30)
        c["z"] = z
        c["sums"] = _suffix_sums(_softplus(z), tri2_ref[...])

    def weights(c):
        nr, rows, p = c["nr"], c["rows"], c["p"]
        carry = jnp.concatenate([carry_ref[2 * p, rows, :], carry_ref[2 * p + 1, rows, :]], axis=0)
        a = jnp.exp(c["z"] - c["sums"][:, :BLK] - carry)
        pv = jnp.dot(a.astype(BF16), v_ref[:, p * BLK:(p + 1) * BLK], preferred_element_type=F32)
        lane = lax.broadcasted_iota(jnp.int32, (nr, BLK), 1)
        c["acc"] = acc_ref[p, rows, :] + jnp.where(lane < SB_HEAD_DIM, pv[:nr], pv[nr:])
        c["carry"] = carry + c["sums"][:, BLK:]

    for (first_blk, n_blk), group, above in cases:
        @pl.when((j >= first_blk) & (j < first_blk + n_blk))
        def _(group=group, above=above):
            todo = [(f, n, masked, p)
                    for masked, blocks in ((True, group), (False, above))
                    for f, n in blocks for p in range(SB_PAIRS_PER_STEP)]
            chunks = []
            for t in range(len(todo) + 2):
                if t < len(todo):
                    chunks.append(scores(*todo[t]))
                if 0 <= t - 1 < len(todo):
                    suffix(chunks[t - 1])
                if 0 <= t - 2 < len(todo):
                    weights(chunks[t - 2])
            for c in chunks:
                nr, rows, p = c["nr"], c["rows"], c["p"]
                acc_ref[p, rows, :] = c["acc"]
                carry_ref[2 * p, rows, :] = c["carry"][:nr]
                carry_ref[2 * p + 1, rows, :] = c["carry"][nr:]

    @pl.when(step == nblk - 1)
    def _():
        for p in range(SB_PAIRS_PER_STEP):
            o_ref[:, p * BLK:(p + 1) * BLK] = acc_ref[p].astype(o_ref.dtype)


def _split_blocks(first, count):
    pieces = pl.cdiv(count, SB_CHUNK_BLOCKS)
    out = []
    for p in range(pieces):
        size = count // pieces + (1 if p < count % pieces else 0)
        out.append((first, size))
        first += size
    return tuple(out)


def _sb_cases(nblk):
    cases = []
    for first in range(0, max(nblk - nblk % SB_MASK_BLOCKS, 1), SB_MASK_BLOCKS):
        n = SB_MASK_BLOCKS if first + 2 * SB_MASK_BLOCKS <= nblk else nblk - first
        cases.append(((first, n), _split_blocks(first, n), _split_blocks(first + n, nblk - first - n)))
    return tuple(cases)


def _sb_prompt(qkv, bias_rows, tri2, batch, nblk, rows):
    lp = nblk * BLK
    width = SB_PAIRS_PER_STEP * BLK
    groups = D_MODEL // width
    return pl.pallas_call(
        functools.partial(_sb_prompt_kernel, cases=_sb_cases(nblk)),
        grid=(batch, groups, nblk),
        in_specs=[
            pl.BlockSpec((lp, width), lambda b, g, s: (b, g)),
            pl.BlockSpec((BLK, width), lambda b, g, s: (b * nblk + nblk - 1 - s, groups + g)),
            pl.BlockSpec((BLK, width), lambda b, g, s: (b * nblk + nblk - 1 - s, 2 * groups + g)),
            pl.BlockSpec((2 * SB_PAIRS_PER_STEP, 1, BLK), lambda b, g, s: (g, 0, 0)),
            pl.BlockSpec((2 * BLK, 2 * BLK), lambda b, g, s: (0, 0)),
        ],
        out_specs=pl.BlockSpec((lp, width), lambda b, g, s: (b, g)),
        out_shape=jax.ShapeDtypeStruct((rows, D_MODEL), BF16),
        scratch_shapes=[pltpu.VMEM((2 * SB_PAIRS_PER_STEP, lp, BLK), BF16),
                        pltpu.VMEM((SB_PAIRS_PER_STEP, lp, BLK), F32),
                        pltpu.VMEM((2 * SB_PAIRS_PER_STEP, lp, BLK), F32)],
        compiler_params=_params("parallel", "parallel", "arbitrary"),
        name="sb_prompt",
    )(qkv, qkv, qkv, bias_rows, tri2)


def _sb_decode_kernel(pt_ref, q_ref, *refs, n_pg):
    del pt_ref
    k_refs, v_refs = refs[:n_pg], refs[n_pg:2 * n_pg]
    bias_ref, tri2_ref, o_ref, acc_ref, carry_ref = refs[2 * n_pg:]
    g = pl.program_id(1)

    @pl.when(g == 0)
    def _():
        acc_ref[...] = jnp.zeros_like(acc_ref)
        carry_ref[...] = jnp.zeros_like(carry_ref)

    lane_d = lax.broadcasted_iota(jnp.int32, (SB_HEADS, D_MODEL), 1)
    head_d = lax.broadcasted_iota(jnp.int32, (SB_HEADS, D_MODEL), 0)
    own = lax.shift_right_logical(lane_d, 6) == head_d
    q = q_ref[...] * (SB_HEAD_DIM ** -0.5)
    q_heads = jnp.where(own, jnp.broadcast_to(q, (SB_HEADS, D_MODEL)), 0.0).astype(BF16)

    z = jnp.concatenate(
        [jnp.dot(q_heads, k_refs[p][...].reshape(D_MODEL, PAGE_SIZE).astype(BF16), preferred_element_type=F32)
         for p in range(n_pg)], axis=0) + jnp.concatenate([bias_ref[...]] * n_pg, axis=0)
    sums = _suffix_sums(_softplus(z), tri2_ref[...])
    carry = carry_ref[...]
    acc = acc_ref[...]
    for p in range(n_pg):
        rows = slice(p * SB_HEADS, (p + 1) * SB_HEADS)
        a = jnp.exp(z[rows] - sums[rows, :PAGE_SIZE] - carry)
        acc = acc + lax.dot_general(a.astype(BF16), v_refs[p][...].reshape(D_MODEL, PAGE_SIZE).astype(BF16),
                                    NT, preferred_element_type=F32)
        carry = carry + sums[rows, PAGE_SIZE:]
    acc_ref[...] = acc
    carry_ref[...] = carry

    @pl.when(g == pl.num_programs(1) - 1)
    def _():
        o_ref[...] = jnp.sum(jnp.where(own, acc, 0.0), axis=0, keepdims=True)


def _sb_decode(q, cache_k, cache_v, layer, page_table, bias_col, tri2):
    n, n_pages = page_table.shape
    n_pg = DEC_PAGES_PER_STEP
    assert n_pages % n_pg == 0
    ck = jnp.transpose(cache_k, (0, 1, 3, 4, 2))
    cv = jnp.transpose(cache_v, (0, 1, 3, 4, 2))

    def page_spec(p):
        return pl.BlockSpec((None, None, SB_HEADS, SB_HEAD_DIM, PAGE_SIZE),
                            lambda s, g, pt: (layer, pt[s, n_pages - 1 - (g * n_pg + p)], 0, 0, 0))

    row_spec = pl.BlockSpec((None, 1, D_MODEL), lambda s, g, pt: (s, 0, 0))
    return pl.pallas_call(
        functools.partial(_sb_decode_kernel, n_pg=n_pg),
        grid_spec=pltpu.PrefetchScalarGridSpec(
            num_scalar_prefetch=1,
            grid=(n, n_pages // n_pg),
            in_specs=[row_spec] + [page_spec(p) for p in range(n_pg)] * 2
                     + [pl.BlockSpec((SB_HEADS, 1), lambda s, g, pt: (0, 0)),
                        pl.BlockSpec((2 * BLK, 2 * BLK), lambda s, g, pt: (0, 0))],
            out_specs=row_spec,
            scratch_shapes=[pltpu.VMEM((SB_HEADS, D_MODEL), F32), pltpu.VMEM((SB_HEADS, PAGE_SIZE), F32)],
        ),
        out_shape=jax.ShapeDtypeStruct((n, 1, D_MODEL), F32),
        compiler_params=_params("parallel", "arbitrary"),
        name="sb_decode",
    )(page_table, q, *([ck] * n_pg), *([cv] * n_pg), bias_col, tri2)


def _retention_tables(n_last):
    log_g = jnp.log(1.0 - 2.0 ** (-5.0 - jnp.arange(RET_HEADS, dtype=F32)))
    idx = jnp.arange(BLK, dtype=F32)
    diff = idx[:, None] - idx[None, :]
    lg = log_g[:, None, None]
    dec = jnp.where(diff >= 0, jnp.exp(lg * jnp.maximum(diff, 0.0)), 0.0)
    qd = jnp.exp(log_g[:, None] * (idx[None, :] + 1.0))[:, :, None]
    n_tok = jnp.array([BLK, n_last], F32)[:, None, None]
    kd = jnp.exp(log_g[None, :, None] * (n_tok - 1.0 - idx[None, None, :]))[..., None]
    cd = jnp.broadcast_to(jnp.exp(log_g[None, :, None] * n_tok)[..., None], (2, RET_HEADS, 1, BLK))
    gam = jnp.broadcast_to(jnp.exp(log_g)[:, None, None], (RET_HEADS, 1, RET_V_DIM))
    return dec, qd, kd, cd, gam


def _rope_tables(pos):
    half = RET_QK_DIM // 2
    inv = ROPE_BASE ** (-jnp.arange(half, dtype=F32) / half)
    ang = pos.astype(F32)[:, None] * inv[None, :]
    return jnp.cos(ang), jnp.sin(ang)


def _with_decode_rows(mixer_out, dec_rows, dec_row0):
    return lax.dynamic_update_slice(mixer_out, dec_rows.astype(mixer_out.dtype), (dec_row0, 0))


def kernel(x_prompt, x_sample, state_ret, cache_k, cache_v, page_table, meta_tokens, w_ret_in, w_ret_out,
           w_sb_qkv, w_sb_out, sb_bias, w_ffn_in, w_ffn_out, ln_mix_g, ln_mix_b, ln_ffn_g, ln_ffn_b):
    batch, seq, d = x_prompt.shape
    n_dec = x_sample.shape[0]
    assert d == D_MODEL and x_sample.shape[1] == 1 and cache_k.shape[2] == PAGE_SIZE
    n_real = N_META + seq
    nblk = pl.cdiv(n_real, BLK)
    lp = nblk * BLK
    rows = batch * lp
    assert rows % ROW_TILE == 0 and n_dec <= lp - n_real
    dec_row0 = (batch - 1) * lp + n_real
    dec_rows = slice(dec_row0, dec_row0 + n_dec)
    past_len = page_table.shape[1] * PAGE_SIZE

    w_ret_in, w_ret_out, w_sb_qkv, w_sb_out, w_ffn_in, w_ffn_out = (
        w.astype(BF16) for w in (w_ret_in, w_ret_out, w_sb_qkv, w_sb_out, w_ffn_in, w_ffn_out))

    pad = lax.dynamic_update_slice(jnp.zeros((batch, lp - n_real, d), F32),
                                   x_sample.reshape(1, n_dec, d), (batch - 1, 0, 0))
    h = jnp.concatenate([jnp.broadcast_to(meta_tokens[None].astype(F32), (batch, N_META, d)), x_prompt, pad],
                        axis=1).reshape(rows, d)

    dec, qd, kd, cd, gam = _retention_tables(n_real - (nblk - 1) * BLK)
    cos_p, sin_p = _rope_tables(jnp.arange(lp))
    cos_s, sin_s = _rope_tables(past_len + jnp.arange(1))
    idx = jnp.arange(BLK)
    tri = jnp.concatenate([idx[:, None] >= idx[None, :], jnp.ones((BLK, BLK), bool)], axis=1)
    tri2 = jnp.concatenate([tri, tri], axis=0).astype(BF16)

    n_ret, n_sb = (DEPTH + 1) // 2, DEPTH // 2
    ret_p = ret_s = None
    kv_rows = ()
    for i in range(DEPTH):
        j = i // 2
        if i % 2 == 0:
            (qkvg,) = _proj(h, w_ret_in[j], [(BF16, 0, w_ret_in.shape[2], False)], f"ret_in_{j}")
            mix_p, ret_p = _ret_prompt(qkvg, cos_p, sin_p, dec, qd, kd, cd, batch, nblk, rows, n_real,
                                       j, n_ret, ret_p)
            mix_s, ret_s = _ret_decode(qkvg[dec_rows].astype(F32), state_ret,
                                       cos_s, sin_s, gam, j, n_ret, ret_s)
            mix = _with_decode_rows(mix_p, mix_s.reshape(n_dec, -1), dec_row0)
            w_mix = w_ret_out[j]
        else:
            qkv_b, *kv_rows = _proj(h, w_sb_qkv[j],
                                    [(BF16, 0, 3 * d, False), (F32, d, d, True), (F32, 2 * d, d, True)],
                                    f"sb_qkv_{j}", j, n_sb, tuple(kv_rows))
            bias = sb_bias[j].astype(F32)
            mix_p = _sb_prompt(qkv_b, jnp.broadcast_to(bias[:, None, None], (SB_HEADS, 1, BLK)), tri2,
                               batch, nblk, rows)
            q_s = qkv_b[dec_rows, :d].astype(F32)
            mix_s = _sb_decode(q_s.reshape(n_dec, 1, d), cache_k, cache_v, j, page_table,
                               bias.reshape(SB_HEADS, 1), tri2)
            mix = _with_decode_rows(mix_p, mix_s.reshape(n_dec, d), dec_row0)
            w_mix = w_sb_out[j]
        h = _mix_ffn(mix, w_mix, h, ln_mix_g[i], ln_mix_b[i], w_ffn_in[i], w_ffn_out[i],
                     ln_ffn_g[i], ln_ffn_b[i], f"mix_ffn_{i}")

    y_prompt = h.reshape(batch, lp, d)[:, N_META:n_real]
    y_sample = h[dec_rows].reshape(n_dec, 1, d)
    prompt_rows = lambda t: t.reshape(n_sb, batch, lp, SB_HEADS, SB_HEAD_DIM)[:, :, :n_real]
    decode_rows = lambda t: t[:, dec_rows].reshape(n_sb, n_dec, 1, SB_HEADS, SB_HEAD_DIM)
    k_rows, v_rows = kv_rows
    return (y_prompt, y_sample, ret_p, prompt_rows(k_rows), prompt_rows(v_rows),
            ret_s, decode_rows(k_rows), decode_rows(v_rows))
```

```python
import functools

import jax
import jax.numpy as jnp
from jax import lax
from jax.experimental import pallas as pl
from jax.experimental.pallas import tpu as pltpu

D_MODEL = 1024
N_META = 16
RET_HEADS = 4
RET_QK_DIM = D_MODEL // RET_HEADS
RET_V_DIM = 2 * RET_QK_DIM
ROPE_BASE = 10000.0
SB_HEADS = 16
SB_HEAD_DIM = D_MODEL // SB_HEADS
PAGE_SIZE = 128
DEPTH = 4
DEEPNORM_ALPHA = (2 * DEPTH) ** 0.25
LN_EPS = 1e-5

BLK = 128
ROW_TILE = 512
SB_CHUNK_BLOCKS = 5
SB_MASK_BLOCKS = 2
SB_PAIRS_PER_STEP = 4
DEC_PAGES_PER_STEP = 8
VMEM_LIMIT = 56 * 1024 * 1024

F32 = jnp.float32
BF16 = jnp.bfloat16
NT = (((1,), (1,)), ((), ()))
TN = (((0,), (0,)), ((), ()))


def _params(*sem):
    return pltpu.CompilerParams(dimension_semantics=sem, vmem_limit_bytes=VMEM_LIMIT)


def _resident(shape):
    nd = len(shape)
    return pl.BlockSpec(shape, lambda *_: (0,) * nd, pipeline_mode=pl.Buffered(1))


def _layer_norm(y, g, b):
    mu = jnp.mean(y, axis=-1, keepdims=True)
    d = y - mu
    var = jnp.mean(d * d, axis=-1, keepdims=True)
    return d * lax.rsqrt(var + LN_EPS) * g + b


def _silu(x):
    return x * jax.nn.sigmoid(x)


def _proj_kernel(x_ref, w_ref, *refs, col_chunk, out_cols, n_unread):
    o_refs = refs[n_unread:]
    xb = x_ref[...].astype(BF16)
    for c in range(w_ref.shape[1] // col_chunk):
        lo = c * col_chunk
        acc = jnp.dot(xb, w_ref[:, lo:lo + col_chunk], preferred_element_type=F32)
        for o_ref, (first, width) in zip(o_refs, out_cols):
            if first <= lo < first + width:
                o_ref[:, lo - first:lo - first + col_chunk] = acc.astype(o_ref.dtype)


def _proj(x, w, outs, name, layer=0, n_layers=1, prev=()):
    rows, k = x.shape
    n = w.shape[1]
    col_chunk = 512
    assert all(first % col_chunk == 0 and width % col_chunk == 0 for _, first, width, _ in outs)
    out_specs, out_shape = [], []
    for dt, _, wd, per_layer in outs:
        if per_layer:
            out_specs.append(pl.BlockSpec((None, ROW_TILE, wd), lambda i: (layer, i, 0)))
            out_shape.append(jax.ShapeDtypeStruct((n_layers, rows, wd), dt))
        else:
            out_specs.append(pl.BlockSpec((ROW_TILE, wd), lambda i: (i, 0)))
            out_shape.append(jax.ShapeDtypeStruct((rows, wd), dt))
    slots = [o for o, out in enumerate(outs) if out[3]]
    assert len(prev) in (0, len(slots))
    kern = functools.partial(_proj_kernel, col_chunk=col_chunk, out_cols=tuple((f, wd) for _, f, wd, _ in outs),
                             n_unread=len(prev))
    return pl.pallas_call(
        kern,
        grid=(rows // ROW_TILE,),
        in_specs=[pl.BlockSpec((ROW_TILE, k), lambda i: (i, 0)), _resident((k, n))]
                 + [pl.BlockSpec(memory_space=pl.ANY)] * len(prev),
        out_specs=out_specs,
        out_shape=out_shape,
        input_output_aliases={2 + p: slots[p] for p in range(len(prev))},
        compiler_params=_params("parallel"),
        name=name,
    )(x, w, *prev)


def _mix_ffn_kernel(m_ref, wm_ref, h_ref, g1_ref, b1_ref, wi_ref, wo_ref, g2_ref, b2_ref, o_ref, acc_ref,
                    *, hid_chunk):
    mix = jnp.dot(m_ref[...], wm_ref[...], preferred_element_type=F32)
    x = _layer_norm(DEEPNORM_ALPHA * h_ref[...] + mix, g1_ref[...], b1_ref[...])
    xb = x.astype(BF16)
    hidden = wo_ref.shape[0]
    for c in range(hidden // hid_chunk):
        lo = c * hid_chunk
        a = jnp.dot(xb, wi_ref[:, lo:lo + hid_chunk], preferred_element_type=F32)
        b = jnp.dot(xb, wi_ref[:, hidden + lo:hidden + lo + hid_chunk], preferred_element_type=F32)
        part = jnp.dot((_silu(a) * b).astype(BF16), wo_ref[lo:lo + hid_chunk, :], preferred_element_type=F32)
        if c == 0:
            acc_ref[...] = part
        else:
            acc_ref[...] += part
    o_ref[...] = _layer_norm(DEEPNORM_ALPHA * x + acc_ref[...], g2_ref[...], b2_ref[...])


def _mix_ffn(mix, w_mix, h, g1, b1, w_in, w_out, g2, b2, name):
    rows, k = mix.shape
    d = w_mix.shape[1]
    hidden = w_out.shape[0]
    row_spec = pl.BlockSpec((ROW_TILE, d), lambda i: (i, 0))
    vec = lambda v: v.reshape(1, d)
    return pl.pallas_call(
        functools.partial(_mix_ffn_kernel, hid_chunk=256),
        grid=(rows // ROW_TILE,),
        in_specs=[pl.BlockSpec((ROW_TILE, k), lambda i: (i, 0)), _resident((k, d)), row_spec,
                  _resident((1, d)), _resident((1, d)),
                  _resident((d, 2 * hidden)), _resident((hidden, d)), _resident((1, d)), _resident((1, d))],
        out_specs=row_spec,
        out_shape=jax.ShapeDtypeStruct((rows, d), F32),
        scratch_shapes=[pltpu.VMEM((ROW_TILE, d), F32)],
        compiler_params=_params("parallel"),
        name=name,
    )(mix, w_mix, h, vec(g1), vec(b1), w_in, w_out, vec(g2), vec(b2))


def _rotate_lanes(x, cos, sin):
    half = x.shape[-1] // 2
    x1, x2 = x[:, :half], x[:, half:]
    return jnp.concatenate([x1 * cos - x2 * sin, x1 * sin + x2 * cos], axis=-1)


def _rms_gate(o, g):
    on = o * lax.rsqrt(jnp.mean(o * o, axis=-1, keepdims=True) + LN_EPS)
    return _silu(g) * on


def _state_slots(layer, n_layers, slot_shape, prev):
    zeros = (0,) * (len(slot_shape) - 1)
    spec = pl.BlockSpec((None, None) + slot_shape[1:], lambda i, *_: (layer, i) + zeros)
    shape = jax.ShapeDtypeStruct((n_layers,) + slot_shape, F32)
    extra_in = [] if prev is None else [pl.BlockSpec(memory_space=pl.ANY)]
    return spec, shape, extra_in


def _ret_prompt_kernel(x_ref, cos_ref, sin_ref, dec_ref, qd_ref, kd_ref, cd_ref, *rest, n_real):
    o_ref, st_ref, state_ref = rest[-3:]
    c = pl.program_id(1)

    @pl.when(c == 0)
    def _():
        state_ref[...] = jnp.zeros_like(state_ref)

    dk, dv = RET_QK_DIM, RET_V_DIM
    k0, v0, g0 = RET_HEADS * dk, 2 * RET_HEADS * dk, 2 * RET_HEADS * dk + RET_HEADS * dv
    cos, sin = cos_ref[...], sin_ref[...]
    row = c * BLK + lax.broadcasted_iota(jnp.int32, (BLK, 1), 0)

    def scores(hh):
        q = _rotate_lanes(x_ref[:, hh * dk:(hh + 1) * dk].astype(F32), cos, sin)
        k = _rotate_lanes(x_ref[:, k0 + hh * dk:k0 + (hh + 1) * dk].astype(F32), cos, sin) * (dk ** -0.5)
        k = jnp.where(row < n_real, k, 0.0)
        s = lax.dot_general(q.astype(BF16), k.astype(BF16), NT, preferred_element_type=F32) * dec_ref[hh]
        return dict(hh=hh, q=q, k=k, s=s, vb=x_ref[:, v0 + hh * dv:v0 + (hh + 1) * dv], state=state_ref[hh])

    def output(c_):
        hh = c_["hh"]
        o = (jnp.dot(c_["s"].astype(BF16), c_["vb"], preferred_element_type=F32)
             + jnp.dot((c_["q"] * qd_ref[hh]).astype(BF16), c_["state"].astype(BF16),
                       preferred_element_type=F32))
        g = x_ref[:, g0 + hh * dv:g0 + (hh + 1) * dv].astype(F32)
        o_ref[:, hh * dv:(hh + 1) * dv] = _rms_gate(o, g).astype(o_ref.dtype)

    def update(c_):
        hh = c_["hh"]
        state_ref[hh] = cd_ref[hh][:, 0:1] * c_["state"] + lax.dot_general(
            (c_["k"] * kd_ref[hh]).astype(BF16), c_["vb"], TN, preferred_element_type=F32)

    heads = []
    for t in range(RET_HEADS + 2):
        if t < RET_HEADS:
            heads.append(scores(t))
        if 0 <= t - 1 < RET_HEADS:
            output(heads[t - 1])
        if 0 <= t - 2 < RET_HEADS:
            update(heads[t - 2])

    @pl.when(c == pl.num_programs(1) - 1)
    def _():
        st_ref[...] = state_ref[...]


def _ret_prompt(qkvg, cos, sin, dec, qd, kd, cd, batch, nblk, rows, n_real, layer, n_layers, prev):
    h, dk, dv = RET_HEADS, RET_QK_DIM, RET_V_DIM
    last = lambda c: (c == nblk - 1).astype(jnp.int32)
    st_spec, st_shape, extra_in = _state_slots(layer, n_layers, (batch, h, dk, dv), prev)
    return pl.pallas_call(
        functools.partial(_ret_prompt_kernel, n_real=n_real),
        grid=(batch, nblk),
        in_specs=[
            pl.BlockSpec((BLK, qkvg.shape[1]), lambda b, c: (b * nblk + c, 0)),
            pl.BlockSpec((BLK, dk // 2), lambda b, c: (c, 0)),
            pl.BlockSpec((BLK, dk // 2), lambda b, c: (c, 0)),
            pl.BlockSpec((h, BLK, BLK), lambda b, c: (0, 0, 0)),
            pl.BlockSpec((h, BLK, 1), lambda b, c: (0, 0, 0)),
            pl.BlockSpec((None, h, BLK, 1), lambda b, c: (last(c), 0, 0, 0)),
            pl.BlockSpec((None, h, 1, BLK), lambda b, c: (last(c), 0, 0, 0)),
        ] + extra_in,
        out_specs=[pl.BlockSpec((BLK, h * dv), lambda b, c: (b * nblk + c, 0)), st_spec],
        out_shape=[jax.ShapeDtypeStruct((rows, h * dv), BF16), st_shape],
        scratch_shapes=[pltpu.VMEM((h, dk, dv), F32)],
        input_output_aliases={} if prev is None else {7: 1},
        compiler_params=_params("parallel", "arbitrary"),
        name="ret_prompt",
    )(qkvg, cos, sin, dec, qd, kd, cd, *([] if prev is None else [prev]))


def _ret_decode_kernel(q_ref, kr_ref, kc_ref, v_ref, g_ref, cr_ref, sr_ref, cc_ref, sc_ref, gam_ref,
                       st_ref, *rest):
    o_ref, so_ref = rest[-2:]
    dk, dv = RET_QK_DIM, RET_V_DIM
    scale = dk ** -0.5
    half = dk // 2
    cc, sc = cc_ref[...], sc_ref[...]
    for hh in range(RET_HEADS):
        q = _rotate_lanes(q_ref[:, hh * dk:(hh + 1) * dk], cr_ref[...], sr_ref[...])
        k_row = _rotate_lanes(kr_ref[:, hh * dk:(hh + 1) * dk], cr_ref[...], sr_ref[...]) * scale
        kc = kc_ref[hh]
        k1, k2 = kc[:half], kc[half:]
        k_col = jnp.concatenate([k1 * cc - k2 * sc, k1 * sc + k2 * cc], axis=0) * scale
        gam = gam_ref[hh]
        state = st_ref[hh]
        v = v_ref[:, hh * dv:(hh + 1) * dv]
        score = jnp.sum(q * k_row, axis=-1, keepdims=True)
        q_dec = jnp.broadcast_to(q * gam[:, :dk], (8, dk)).astype(BF16)
        o = score * v + jnp.dot(q_dec, state.astype(BF16), preferred_element_type=F32)[0:1]
        so_ref[hh] = gam * state + k_col * v
        o_ref[:, hh * dv:(hh + 1) * dv] = _rms_gate(o, g_ref[:, hh * dv:(hh + 1) * dv])


def _ret_decode(dec_rows, state, cos_row, sin_row, gam, layer, n_layers, prev):
    n = dec_rows.shape[0]
    h, dk, dv = RET_HEADS, RET_QK_DIM, RET_V_DIM
    q = dec_rows[:, :h * dk].reshape(n, 1, h * dk)
    k = dec_rows[:, h * dk:2 * h * dk]
    v = dec_rows[:, 2 * h * dk:2 * h * dk + h * dv].reshape(n, 1, h * dv)
    g = dec_rows[:, 2 * h * dk + h * dv:].reshape(n, 1, h * dv)
    row_qk = pl.BlockSpec((None, 1, h * dk), lambda s: (s, 0, 0))
    row_v = pl.BlockSpec((None, 1, h * dv), lambda s: (s, 0, 0))
    const2 = lambda shape: pl.BlockSpec(shape, lambda s: (0, 0))
    so_spec, so_shape, extra_in = _state_slots(layer, n_layers, (n, h, dk, dv), prev)
    return pl.pallas_call(
        _ret_decode_kernel,
        grid=(n,),
        in_specs=[row_qk, row_qk, pl.BlockSpec((None, h, dk, 1), lambda s: (s, 0, 0, 0)), row_v, row_v,
                  const2((1, dk // 2)), const2((1, dk // 2)), const2((dk // 2, 1)), const2((dk // 2, 1)),
                  pl.BlockSpec((h, 1, dv), lambda s: (0, 0, 0)),
                  pl.BlockSpec((None, None, h, dk, dv), lambda s: (layer, s, 0, 0, 0))] + extra_in,
        out_specs=[row_v, so_spec],
        out_shape=[jax.ShapeDtypeStruct((n, 1, h * dv), F32), so_shape],
        input_output_aliases={} if prev is None else {11: 1},
        compiler_params=_params("parallel"),
        name="ret_decode",
    )(q, k.reshape(n, 1, h * dk), k.reshape(n, h, dk, 1), v, g,
      cos_row, sin_row, cos_row.reshape(-1, 1), sin_row.reshape(-1, 1), gam, state,
      *([] if prev is None else [prev]))


def _softplus(z):
    return jnp.maximum(z, 0.0) + jnp.log(1.0 + jnp.exp(-jnp.abs(z)))


def _suffix_sums(keep, tri2):
    hi = keep.astype(BF16)
    lo = (keep - hi.astype(F32)).astype(BF16)
    return jnp.dot(jnp.concatenate([hi, lo], axis=1), tri2, preferred_element_type=F32)


def _sb_prompt_kernel(q_ref, k_ref, v_ref, bias_ref, tri2_ref, o_ref, qh_ref, acc_ref, carry_ref, *, cases):
    step = pl.program_id(2)
    nblk = pl.num_programs(2)
    j = nblk - 1 - step

    @pl.when(step == 0)
    def _():
        lane = lax.broadcasted_iota(jnp.int32, (q_ref.shape[0], BLK), 1)
        for p in range(SB_PAIRS_PER_STEP):
            qf = q_ref[:, p * BLK:(p + 1) * BLK].astype(F32) * (SB_HEAD_DIM ** -0.5)
            qh_ref[2 * p] = jnp.where(lane < SB_HEAD_DIM, qf, 0.0).astype(BF16)
            qh_ref[2 * p + 1] = jnp.where(lane >= SB_HEAD_DIM, qf, 0.0).astype(BF16)
        acc_ref[...] = jnp.zeros_like(acc_ref)
        carry_ref[...] = jnp.zeros_like(carry_ref)

    def scores(first_blk, n_blk, masked, p):
        r0, nr = first_blk * BLK, n_blk * BLK
        rows = slice(r0, r0 + nr)
        q2 = jnp.concatenate([qh_ref[2 * p, rows, :], qh_ref[2 * p + 1, rows, :]], axis=0)
        z = lax.dot_general(q2, k_ref[:, p * BLK:(p + 1) * BLK], NT, preferred_element_type=F32)
        return dict(r0=r0, nr=nr, rows=rows, masked=masked, p=p, z=z)

    def suffix(c):
        nr, p = c["nr"], c["p"]
        z = c["z"] + jnp.concatenate([jnp.broadcast_to(bias_ref[2 * p], (nr, BLK)),
                                      jnp.broadcast_to(bias_ref[2 * p + 1], (nr, BLK))], axis=0)
        if c["masked"]:
            lane = lax.broadcasted_iota(jnp.int32, (nr, BLK), 1)
            earlier = (lane - lax.broadcasted_iota(jnp.int32, (nr, BLK), 0)) < (c["r0"] - j * BLK)
            z = jnp.where(jnp.concatenate([earlier, earlier], axis=0), z, -1e30)
        c["z"] = z
        c["sums"] = _suffix_sums(_softplus(z), tri2_ref[...])

    def weights(c):
        nr, rows, p = c["nr"], c["rows"], c["p"]
        carry = jnp.concatenate([carry_ref[2 * p, rows, :], carry_ref[2 * p + 1, rows, :]], axis=0)
        a = jnp.exp(c["z"] - c["sums"][:, :BLK] - carry)
        pv = jnp.dot(a.astype(BF16), v_ref[:, p * BLK:(p + 1) * BLK], preferred_element_type=F32)
        lane = lax.broadcasted_iota(jnp.int32, (nr, BLK), 1)
        c["acc"] = acc_ref[p, rows, :] + jnp.where(lane < SB_HEAD_DIM, pv[:nr], pv[nr:])
        c["carry"] = carry + c["sums"][:, BLK:]

    for (first_blk, n_blk), group, above in cases:
        @pl.when((j >= first_blk) & (j < first_blk + n_blk))
        def _(group=group, above=above):
            todo = [(f, n, masked, p)
                    for masked, blocks in ((True, group), (False, above))
                    for f, n in blocks for p in range(SB_PAIRS_PER_STEP)]
            chunks = []
            for t in range(len(todo) + 2):
                if t < len(todo):
                    chunks.append(scores(*todo[t]))
                if 0 <= t - 1 < len(todo):
                    suffix(chunks[t - 1])
                if 0 <= t - 2 < len(todo):
                    weights(chunks[t - 2])
            for c in chunks:
                nr, rows, p = c["nr"], c["rows"], c["p"]
                acc_ref[p, rows, :] = c["acc"]
                carry_ref[2 * p, rows, :] = c["carry"][:nr]
                carry_ref[2 * p + 1, rows, :] = c["carry"][nr:]

    @pl.when(step == nblk - 1)
    def _():
        for p in range(SB_PAIRS_PER_STEP):
            o_ref[:, p * BLK:(p + 1) * BLK] = acc_ref[p].astype(o_ref.dtype)


def _split_blocks(first, count):
    pieces = pl.cdiv(count, SB_CHUNK_BLOCKS)
    out = []
    for p in range(pieces):
        size = count // pieces + (1 if p < count % pieces else 0)
        out.append((first, size))
        first += size
    return tuple(out)


def _sb_cases(nblk):
    cases = []
    for first in range(0, max(nblk - nblk % SB_MASK_BLOCKS, 1), SB_MASK_BLOCKS):
        n = SB_MASK_BLOCKS if first + 2 * SB_MASK_BLOCKS <= nblk else nblk - first
        cases.append(((first, n), _split_blocks(first, n), _split_blocks(first + n, nblk - first - n)))
    return tuple(cases)


def _sb_prompt(qkv, bias_rows, tri2, batch, nblk, rows):
    lp = nblk * BLK
    width = SB_PAIRS_PER_STEP * BLK
    groups = D_MODEL // width
    return pl.pallas_call(
        functools.partial(_sb_prompt_kernel, cases=_sb_cases(nblk)),
        grid=(batch, groups, nblk),
        in_specs=[
            pl.BlockSpec((lp, width), lambda b, g, s: (b, g)),
            pl.BlockSpec((BLK, width), lambda b, g, s: (b * nblk + nblk - 1 - s, groups + g)),
            pl.BlockSpec((BLK, width), lambda b, g, s: (b * nblk + nblk - 1 - s, 2 * groups + g)),
            pl.BlockSpec((2 * SB_PAIRS_PER_STEP, 1, BLK), lambda b, g, s: (g, 0, 0)),
            pl.BlockSpec((2 * BLK, 2 * BLK), lambda b, g, s: (0, 0)),
        ],
        out_specs=pl.BlockSpec((lp, width), lambda b, g, s: (b, g)),
        out_shape=jax.ShapeDtypeStruct((rows, D_MODEL), BF16),
        scratch_shapes=[pltpu.VMEM((2 * SB_PAIRS_PER_STEP, lp, BLK), BF16),
                        pltpu.VMEM((SB_PAIRS_PER_STEP, lp, BLK), F32),
                        pltpu.VMEM((2 * SB_PAIRS_PER_STEP, lp, BLK), F32)],
        compiler_params=_params("parallel", "parallel", "arbitrary"),
        name="sb_prompt",
    )(qkv, qkv, qkv, bias_rows, tri2)


def _sb_decode_kernel(pt_ref, q_ref, *refs, n_pg):
    del pt_ref
    k_refs, v_refs = refs[:n_pg], refs[n_pg:2 * n_pg]
    bias_ref, tri2_ref, o_ref, acc_ref, carry_ref = refs[2 * n_pg:]
    g = pl.program_id(1)

    @pl.when(g == 0)
    def _():
        acc_ref[...] = jnp.zeros_like(acc_ref)
        carry_ref[...] = jnp.zeros_like(carry_ref)

    lane_d = lax.broadcasted_iota(jnp.int32, (SB_HEADS, D_MODEL), 1)
    head_d = lax.broadcasted_iota(jnp.int32, (SB_HEADS, D_MODEL), 0)
    own = lax.shift_right_logical(lane_d, 6) == head_d
    q = q_ref[...] * (SB_HEAD_DIM ** -0.5)
    q_heads = jnp.where(own, jnp.broadcast_to(q, (SB_HEADS, D_MODEL)), 0.0).astype(BF16)

    z = jnp.concatenate(
        [jnp.dot(q_heads, k_refs[p][...].reshape(D_MODEL, PAGE_SIZE).astype(BF16), preferred_element_type=F32)
         for p in range(n_pg)], axis=0) + jnp.concatenate([bias_ref[...]] * n_pg, axis=0)
    sums = _suffix_sums(_softplus(z), tri2_ref[...])
    carry = carry_ref[...]
    acc = acc_ref[...]
    for p in range(n_pg):
        rows = slice(p * SB_HEADS, (p + 1) * SB_HEADS)
        a = jnp.exp(z[rows] - sums[rows, :PAGE_SIZE] - carry)
        acc = acc + lax.dot_general(a.astype(BF16), v_refs[p][...].reshape(D_MODEL, PAGE_SIZE).astype(BF16),
                                    NT, preferred_element_type=F32)
        carry = carry + sums[rows, PAGE_SIZE:]
    acc_ref[...] = acc
    carry_ref[...] = carry

    @pl.when(g == pl.num_programs(1) - 1)
    def _():
        o_ref[...] = jnp.sum(jnp.where(own, acc, 0.0), axis=0, keepdims=True)


def _sb_decode(q, cache_k, cache_v, layer, page_table, bias_col, tri2):
    n, n_pages = page_table.shape
    n_pg = DEC_PAGES_PER_STEP
    assert n_pages % n_pg == 0
    ck = jnp.transpose(cache_k, (0, 1, 3, 4, 2))
    cv = jnp.transpose(cache_v, (0, 1, 3, 4, 2))

    def page_spec(p):
        return pl.BlockSpec((None, None, SB_HEADS, SB_HEAD_DIM, PAGE_SIZE),
                            lambda s, g, pt: (layer, pt[s, n_pages - 1 - (g * n_pg + p)], 0, 0, 0))

    row_spec = pl.BlockSpec((None, 1, D_MODEL), lambda s, g, pt: (s, 0, 0))
    return pl.pallas_call(
        functools.partial(_sb_decode_kernel, n_pg=n_pg),
        grid_spec=pltpu.PrefetchScalarGridSpec(
            num_scalar_prefetch=1,
            grid=(n, n_pages // n_pg),
            in_specs=[row_spec] + [page_spec(p) for p in range(n_pg)] * 2
                     + [pl.BlockSpec((SB_HEADS, 1), lambda s, g, pt: (0, 0)),
                        pl.BlockSpec((2 * BLK, 2 * BLK), lambda s, g, pt: (0, 0))],
            out_specs=row_spec,
            scratch_shapes=[pltpu.VMEM((SB_HEADS, D_MODEL), F32), pltpu.VMEM((SB_HEADS, PAGE_SIZE), F32)],
        ),
        out_shape=jax.ShapeDtypeStruct((n, 1, D_MODEL), F32),
        compiler_params=_params("parallel", "arbitrary"),
        name="sb_decode",
    )(page_table, q, *([ck] * n_pg), *([cv] * n_pg), bias_col, tri2)


def _retention_tables(n_last):
    log_g = jnp.log(1.0 - 2.0 ** (-5.0 - jnp.arange(RET_HEADS, dtype=F32)))
    idx = jnp.arange(BLK, dtype=F32)
    diff = idx[:, None] - idx[None, :]
    lg = log_g[:, None, None]
    dec = jnp.where(diff >= 0, jnp.exp(lg * jnp.maximum(diff, 0.0)), 0.0)
    qd = jnp.exp(log_g[:, None] * (idx[None, :] + 1.0))[:, :, None]
    n_tok = jnp.array([BLK, n_last], F32)[:, None, None]
    kd = jnp.exp(log_g[None, :, None] * (n_tok - 1.0 - idx[None, None, :]))[..., None]
    cd = jnp.broadcast_to(jnp.exp(log_g[None, :, None] * n_tok)[..., None], (2, RET_HEADS, 1, BLK))
    gam = jnp.broadcast_to(jnp.exp(log_g)[:, None, None], (RET_HEADS, 1, RET_V_DIM))
    return dec, qd, kd, cd, gam


def _rope_tables(pos):
    half = RET_QK_DIM // 2
    inv = ROPE_BASE ** (-jnp.arange(half, dtype=F32) / half)
    ang = pos.astype(F32)[:, None] * inv[None, :]
    return jnp.cos(ang), jnp.sin(ang)


def _with_decode_rows(mixer_out, dec_rows, dec_row0):
    return lax.dynamic_update_slice(mixer_out, dec_rows.astype(mixer_out.dtype), (dec_row0, 0))


def kernel(x_prompt, x_sample, state_ret, cache_k, cache_v, page_table, meta_tokens, w_ret_in, w_ret_out,
           w_sb_qkv, w_sb_out, sb_bias, w_ffn_in, w_ffn_out, ln_mix_g, ln_mix_b, ln_ffn_g, ln_ffn_b):
    batch, seq, d = x_prompt.shape
    n_dec = x_sample.shape[0]
    assert d == D_MODEL and x_sample.shape[1] == 1 and cache_k.shape[2] == PAGE_SIZE
    n_real = N_META + seq
    nblk = pl.cdiv(n_real, BLK)
    lp = nblk * BLK
    rows = batch * lp
    assert rows % ROW_TILE == 0 and n_dec <= lp - n_real
    dec_row0 = (batch - 1) * lp + n_real
    dec_rows = slice(dec_row0, dec_row0 + n_dec)
    past_len = page_table.shape[1] * PAGE_SIZE

    w_ret_in, w_ret_out, w_sb_qkv, w_sb_out, w_ffn_in, w_ffn_out = (
        w.astype(BF16) for w in (w_ret_in, w_ret_out, w_sb_qkv, w_sb_out, w_ffn_in, w_ffn_out))

    pad = lax.dynamic_update_slice(jnp.zeros((batch, lp - n_real, d), F32),
                                   x_sample.reshape(1, n_dec, d), (batch - 1, 0, 0))
    h = jnp.concatenate([jnp.broadcast_to(meta_tokens[None].astype(F32), (batch, N_META, d)), x_prompt, pad],
                        axis=1).reshape(rows, d)

    dec, qd, kd, cd, gam = _retention_tables(n_real - (nblk - 1) * BLK)
    cos_p, sin_p = _rope_tables(jnp.arange(lp))
    cos_s, sin_s = _rope_tables(past_len + jnp.arange(1))
    idx = jnp.arange(BLK)
    tri = jnp.concatenate([idx[:, None] >= idx[None, :], jnp.ones((BLK, BLK), bool)], axis=1)
    tri2 = jnp.concatenate([tri, tri], axis=0).astype(BF16)

    n_ret, n_sb = (DEPTH + 1) // 2, DEPTH // 2
    ret_p = ret_s = None
    kv_rows = ()
    for i in range(DEPTH):
        j = i // 2
        if i % 2 == 0:
            (qkvg,) = _proj(h, w_ret_in[j], [(BF16, 0, w_ret_in.shape[2], False)], f"ret_in_{j}")
            mix_p, ret_p = _ret_prompt(qkvg, cos_p, sin_p, dec, qd, kd, cd, batch, nblk, rows, n_real,
                                       j, n_ret, ret_p)
            mix_s, ret_s = _ret_decode(qkvg[dec_rows].astype(F32), state_ret,
                                       cos_s, sin_s, gam, j, n_ret, ret_s)
            mix = _with_decode_rows(mix_p, mix_s.reshape(n_dec, -1), dec_row0)
            w_mix = w_ret_out[j]
        else:
            qkv_b, *kv_rows = _proj(h, w_sb_qkv[j],
                                    [(BF16, 0, 3 * d, False), (F32, d, d, True), (F32, 2 * d, d, True)],
                                    f"sb_qkv_{j}", j, n_sb, tuple(kv_rows))
            bias = sb_bias[j].astype(F32)
            mix_p = _sb_prompt(qkv_b, jnp.broadcast_to(bias[:, None, None], (SB_HEADS, 1, BLK)), tri2,
                               batch, nblk, rows)
            q_s = qkv_b[dec_rows, :d].astype(F32)
            mix_s = _sb_decode(q_s.reshape(n_dec, 1, d), cache_k, cache_v, j, page_table,
                               bias.reshape(SB_HEADS, 1), tri2)
            mix = _with_decode_rows(mix_p, mix_s.reshape(n_dec, d), dec_row0)
            w_mix = w_sb_out[j]
        h = _mix_ffn(mix, w_mix, h, ln_mix_g[i], ln_mix_b[i], w_ffn_in[i], w_ffn_out[i],
                     ln_ffn_g[i], ln_ffn_b[i], f"mix_ffn_{i}")

    y_prompt = h.reshape(batch, lp, d)[:, N_META:n_real]
    y_sample = h[dec_rows].reshape(n_dec, 1, d)
    prompt_rows = lambda t: t.reshape(n_sb, batch, lp, SB_HEADS, SB_HEAD_DIM)[:, :, :n_real]
    decode_rows = lambda t: t[:, dec_rows].reshape(n_sb, n_dec, 1, SB_HEADS, SB_HEAD_DIM)
    k_rows, v_rows = kv_rows
    return (y_prompt, y_sample, ret_p, prompt_rows(k_rows), prompt_rows(v_rows),
            ret_s, decode_rows(k_rows), decode_rows(v_rows))
```
